```python
import jax
import jax.numpy as jnp
from jax import lax
import numpy as np

D_MODEL = 1024
BATCH = 2
SEQ = 16384
DEPTH = 4

GRID_W = 64
CTX_LEN = 256
RET_HEADS = 8
RET_DK = 64
RET_DV = 64
QK_WIDTH = RET_HEADS * RET_DK
RET_WIDTH = RET_HEADS * RET_DV
CONV_CH = 512
CONV_WIDTH = 31
MIX_WIDTH = RET_WIDTH + CONV_CH
D_FF = 4 * D_MODEL
CHUNK = 128
ROPE_BASE = 10000.0
LN_EPS = 1e-5
DEEPNORM_ALPHA = (2 * DEPTH) ** 0.25
DEEPNORM_BETA = (8 * DEPTH) ** -0.25
Q_OFF = 0
K_OFF = Q_OFF + QK_WIDTH
V_OFF = K_OFF + QK_WIDTH
G_OFF = V_OFF + RET_WIDTH
A_OFF = G_OFF + RET_WIDTH
B_OFF = A_OFF + CONV_CH
IN_WIDTH = B_OFF + CONV_CH

kernel_name = 'hybrid_retention_conformer_dit'


def layer_norm(x, w, b):
    xf = x.astype(jnp.float32)
    mu = jnp.mean(xf, axis=-1, keepdims=True)
    var = jnp.mean(jnp.square(xf - mu), axis=-1, keepdims=True)
    y = (xf - mu) * lax.rsqrt(var + LN_EPS)
    return (y * w.astype(jnp.float32) + b.astype(jnp.float32)).astype(x.dtype)


def modulate(h, shift, scale):
    return h * (1.0 + scale) + shift


def adaln(cond, w_ada, b_ada):
    return jnp.split(jax.nn.silu(cond) @ w_ada + b_ada, 6, axis=-1)


def rope_tables(n):
    rows = n // GRID_W
    row = jnp.repeat(jnp.arange(rows, dtype=jnp.float32), GRID_W)
    col = jnp.tile(jnp.arange(GRID_W, dtype=jnp.float32), rows)
    n_freq = RET_DK // 4
    inv = ROPE_BASE ** (-jnp.arange(n_freq, dtype=jnp.float32) / n_freq)
    ang = jnp.concatenate([row[:, None] * inv, col[:, None] * inv], axis=-1)
    return jnp.cos(ang), jnp.sin(ang)


def apply_rope(t, cos, sin):
    c = cos.astype(t.dtype)
    s = sin.astype(t.dtype)
    t1 = t[..., 0::2]
    t2 = t[..., 1::2]
    return jnp.stack([t1 * c - t2 * s, t1 * s + t2 * c], axis=-1).reshape(t.shape)


def split_heads(t, d):
    b, n, _ = t.shape
    return t.reshape(b, n, RET_HEADS, d).transpose(0, 2, 1, 3)


def flip_seq(t):
    return t[:, :, ::-1]


def retention_scan(q, k, v, log_gamma, s0):
    b, h, n, _ = q.shape
    dv = v.shape[-1]
    nc = n // CHUNK
    lg = log_gamma.astype(jnp.float32)[:, None]
    idx = jnp.arange(CHUNK, dtype=jnp.float32)
    diff = idx[:, None] - idx[None, :]
    intra = jnp.where(diff >= 0, jnp.exp(lg[:, :, None] * jnp.maximum(diff, 0.0)), 0.0)
    q_dec = jnp.exp(lg * (idx + 1.0))
    k_dec = jnp.exp(lg * (CHUNK - 1.0 - idx))
    c_dec = jnp.exp(lg * CHUNK)

    def chunks(t):
        return t.astype(jnp.float32).reshape(b, h, nc, CHUNK, t.shape[-1]).transpose(2, 0, 1, 3, 4)

    def step(s, qkv):
        qc, kc, vc = qkv
        scores = jnp.einsum('bhid,bhjd->bhij', qc, kc) * intra
        o = (jnp.einsum('bhij,bhjv->bhiv', scores, vc)
             + jnp.einsum('bhid,bhdv->bhiv', qc * q_dec[..., None], s))
        s = s * c_dec[..., None] + jnp.einsum('bhjd,bhjv->bhdv', kc * k_dec[..., None], vc)
        return s, o

    s_fin, o = lax.scan(step, s0.astype(jnp.float32), (chunks(q), chunks(k), chunks(v)))
    return o.transpose(1, 2, 0, 3, 4).reshape(b, h, n, dv), s_fin


def final_state(k, v, log_gamma):
    length = k.shape[2]
    lg = log_gamma.astype(jnp.float32)[:, None]
    w = jnp.exp(lg * (length - 1.0 - jnp.arange(length, dtype=jnp.float32)))
    return jnp.einsum('bhjd,bhjv->bhdv', k.astype(jnp.float32) * w[..., None], v.astype(jnp.float32))


def retention_readout(o, g, gn_w, gn_b):
    mu = jnp.mean(o, axis=-1, keepdims=True)
    var = jnp.mean(jnp.square(o - mu), axis=-1, keepdims=True)
    on = (o - mu) * lax.rsqrt(var + LN_EPS)
    b, h, n, dv = on.shape
    on = on.transpose(0, 2, 1, 3).reshape(b, n, h * dv)
    on = (on * gn_w.astype(jnp.float32) + gn_b.astype(jnp.float32)).astype(g.dtype)
    return jax.nn.silu(g) * on


def conformer_conv(a, gate, conv_w, conv_b, ln_w, ln_b):
    u = a * jax.nn.sigmoid(gate)
    u = lax.conv_general_dilated(
        u, conv_w[:, None, :], window_strides=(1,),
        padding=[(CONV_WIDTH // 2, CONV_WIDTH // 2)],
        dimension_numbers=('NWC', 'WIO', 'NWC'),
        feature_group_count=CONV_CH) + conv_b
    return jax.nn.silu(layer_norm(u, ln_w, ln_b))


def mixer_output(o_ret, p, gn_w, gn_b, conv_w, conv_b, cln_w, cln_b, w_out):
    ret = retention_readout(o_ret, p[..., G_OFF:A_OFF], gn_w, gn_b)
    conv = conformer_conv(p[..., A_OFF:B_OFF], p[..., B_OFF:IN_WIDTH], conv_w, conv_b, cln_w, cln_b)
    return jnp.concatenate([ret, conv], axis=-1) @ w_out


def mlp_sublayer(h, shift, scale, gate, w1, w2, ln_w, ln_b):
    y = jnp.square(jax.nn.relu(modulate(h, shift, scale) @ w1)) @ w2
    return layer_norm(DEEPNORM_ALPHA * h + gate * y, ln_w, ln_b)


def setup_inputs(seed: int = 0) -> dict:
    key = jax.random.key(seed)
    ks = jax.random.split(key, 22)
    f32 = jnp.float32

    def nrm(k, shape, s):
        return jax.random.normal(k, shape, f32) * s

    base_rate = jnp.log(-jnp.log1p(-(2.0 ** (-5.0 - jnp.arange(RET_HEADS, dtype=f32)))))
    v_scale = jnp.ones((IN_WIDTH,), f32).at[V_OFF:G_OFF].set(DEEPNORM_BETA)
    return {
        'x': nrm(ks[0], (BATCH, SEQ, D_MODEL), 1.0),
        'c': nrm(ks[1], (BATCH, D_MODEL), 1.0),
        'ctx': nrm(ks[2], (BATCH, CTX_LEN, D_MODEL), 1.0),
        'c_ctx': nrm(ks[3], (D_MODEL,), 1.0),
        'w_ada': nrm(ks[4], (DEPTH, D_MODEL, 6 * D_MODEL), 0.5 * D_MODEL ** -0.5),
        'b_ada': nrm(ks[5], (DEPTH, 6 * D_MODEL), 0.01),
        'w_in': nrm(ks[6], (DEPTH, D_MODEL, IN_WIDTH), D_MODEL ** -0.5) * v_scale,
        'ret_log_rate_fwd': base_rate + nrm(ks[7], (DEPTH, RET_HEADS), 0.1),
        'ret_log_rate_bwd': base_rate + nrm(ks[8], (DEPTH, RET_HEADS), 0.1),
        'ret_gn_w': 1.0 + nrm(ks[9], (DEPTH, RET_WIDTH), 0.02),
        'ret_gn_b': nrm(ks[10], (DEPTH, RET_WIDTH), 0.02),
        'conv_w': nrm(ks[11], (DEPTH, CONV_WIDTH, CONV_CH), CONV_WIDTH ** -0.5),
        'conv_b': nrm(ks[12], (DEPTH, CONV_CH), 0.02),
        'conv_ln_w': 1.0 + nrm(ks[13], (DEPTH, CONV_CH), 0.02),
        'conv_ln_b': nrm(ks[14], (DEPTH, CONV_CH), 0.02),
        'w_out': nrm(ks[15], (DEPTH, MIX_WIDTH, D_MODEL), DEEPNORM_BETA * MIX_WIDTH ** -0.5),
        'ln1_w': 1.0 + nrm(ks[16], (DEPTH, D_MODEL), 0.02),
        'ln1_b': nrm(ks[17], (DEPTH, D_MODEL), 0.02),
        'w_ff1': nrm(ks[18], (DEPTH, D_MODEL, D_FF), D_MODEL ** -0.5),
        'w_ff2': nrm(ks[19], (DEPTH, D_FF, D_MODEL), DEEPNORM_BETA * D_FF ** -0.5),
        'ln2_w': 1.0 + nrm(ks[20], (DEPTH, D_MODEL), 0.02),
        'ln2_b': nrm(ks[21], (DEPTH, D_MODEL), 0.02),
    }


def reference(x, c, ctx, c_ctx, w_ada, b_ada, w_in, ret_log_rate_fwd, ret_log_rate_bwd,
              ret_gn_w, ret_gn_b, conv_w, conv_b, conv_ln_w, conv_ln_b, w_out,
              ln1_w, ln1_b, w_ff1, w_ff2, ln2_w, ln2_b):
    n = x.shape[1]
    cos, sin = rope_tables(n)
    zero_state = jnp.zeros((x.shape[0], RET_HEADS, RET_DK, RET_DV), jnp.float32)

    for l in range(DEPTH):
        last = l == DEPTH - 1
        sh1, sc1, g1, sh2, sc2, g2 = [t[:, None, :] for t in adaln(c, w_ada[l], b_ada[l])]
        csh1, csc1, cg1, csh2, csc2, cg2 = adaln(c_ctx, w_ada[l], b_ada[l])
        lg_f = -jnp.exp(ret_log_rate_fwd[l])
        lg_b = -jnp.exp(ret_log_rate_bwd[l])

        hc = modulate(ctx, csh1, csc1)
        if last:
            pkv = hc @ w_in[l][:, K_OFF:G_OFF]
            kc = split_heads(pkv[..., :QK_WIDTH], RET_DK)
            vc = split_heads(pkv[..., QK_WIDTH:], RET_DV)
            s_f = final_state(kc, vc, lg_f)
            s_b = final_state(flip_seq(kc), flip_seq(vc), lg_b)
        else:
            pc = hc @ w_in[l]
            qc = split_heads(pc[..., Q_OFF:K_OFF], RET_DK)
            kc = split_heads(pc[..., K_OFF:V_OFF], RET_DK) * (RET_DK ** -0.5)
            vc = split_heads(pc[..., V_OFF:G_OFF], RET_DV)
            oc_f, s_f = retention_scan(qc, kc, vc, lg_f, zero_state)
            oc_b, s_b = retention_scan(flip_seq(qc), flip_seq(kc), flip_seq(vc), lg_b, zero_state)
            mix_c = mixer_output(oc_f + flip_seq(oc_b), pc, ret_gn_w[l], ret_gn_b[l],
                                 conv_w[l], conv_b[l], conv_ln_w[l], conv_ln_b[l], w_out[l])

        p = modulate(x, sh1, sc1) @ w_in[l]
        q = apply_rope(split_heads(p[..., Q_OFF:K_OFF], RET_DK), cos, sin)
        k = apply_rope(split_heads(p[..., K_OFF:V_OFF], RET_DK), cos, sin) * (RET_DK ** -0.5)
        v = split_heads(p[..., V_OFF:G_OFF], RET_DV)
        o_f, _ = retention_scan(q, k, v, lg_f, s_f)
        o_b, _ = retention_scan(flip_seq(q), flip_seq(k), flip_seq(v), lg_b, s_b)
        mix_x = mixer_output(o_f + flip_seq(o_b), p, ret_gn_w[l], ret_gn_b[l],
                             conv_w[l], conv_b[l], conv_ln_w[l], conv_ln_b[l], w_out[l])
        x = layer_norm(DEEPNORM_ALPHA * x + g1 * mix_x, ln1_w[l], ln1_b[l])
        x = mlp_sublayer(x, sh2, sc2, g2, w_ff1[l], w_ff2[l], ln2_w[l], ln2_b[l])

        if not last:
            ctx = layer_norm(DEEPNORM_ALPHA * ctx + cg1 * mix_c, ln1_w[l], ln1_b[l])
            ctx = mlp_sublayer(ctx, csh2, csc2, cg2, w_ff1[l], w_ff2[l], ln2_w[l], ln2_b[l])

    return x
```

```python
import functools

import jax
import jax.numpy as jnp
from jax import lax
from jax.experimental import pallas as pl
from jax.experimental.pallas import tpu as pltpu

D_MODEL = 1024
DEPTH = 4
GRID_W = 64
RET_HEADS = 8
HEAD_DIM = 64
RET_WIDTH = RET_HEADS * HEAD_DIM
CONV_CH = 512
CONV_WIDTH = 31
CONV_HALO = 16
IN_WIDTH = 3072
D_FF = 4 * D_MODEL
CHUNK = 128
ROPE_BASE = 10000.0
LN_EPS = 1e-5
ALPHA = (2 * DEPTH) ** 0.25
LANES = 128
N_PAIRS = RET_HEADS // 2
FF_BLOCK = 1024
CONV_ROWS = 128
SUBLANES = 8
VMEM_LIMIT = 56 * 1024 * 1024

F32 = jnp.float32
BF16 = jnp.bfloat16


def _resident(shape):
    zeros = (0,) * len(shape)
    return pl.BlockSpec(shape, lambda *_: zeros, pipeline_mode=pl.Buffered(1))


def _params():
    return pltpu.CompilerParams(dimension_semantics=("arbitrary", "arbitrary"),
                                vmem_limit_bytes=VMEM_LIMIT)


def _layer_norm(x, w, b):
    mu = jnp.mean(x, axis=-1, keepdims=True)
    d = x - mu
    var = jnp.mean(d * d, axis=-1, keepdims=True)
    return d * lax.rsqrt(var + LN_EPS) * w + b


def _pair_masks(rows):
    lane = lax.broadcasted_iota(jnp.int32, (rows, LANES), 1)
    return lane < HEAD_DIM, lane >= HEAD_DIM


def _block_diag_mask():
    r = lax.broadcasted_iota(jnp.int32, (LANES, LANES), 0)
    c = lax.broadcasted_iota(jnp.int32, (LANES, LANES), 1)
    return ((r < HEAD_DIM) == (c < HEAD_DIM)).astype(F32)


def _stack_block_diag(t):
    left, right = _pair_masks(t.shape[0])
    zero = jnp.zeros_like(t)
    return jnp.concatenate([jnp.where(left, t, zero), jnp.where(right, t, zero)], axis=0)


def _adaln_kernel(cond_ref, w_ref, b_ref, o_ref):
    cond = cond_ref[...]
    act = cond * jax.nn.sigmoid(cond)
    o_ref[0] = jnp.dot(act, w_ref[0], preferred_element_type=F32,
                       precision=lax.Precision.HIGHEST) + b_ref[0]


def _adaln(cond, w_ada, b_ada):
    width = w_ada.shape[-1]
    block = 1536
    return pl.pallas_call(
        _adaln_kernel,
        grid=(DEPTH, width // block),
        in_specs=[pl.BlockSpec((8, D_MODEL), lambda l, j: (0, 0)),
                  pl.BlockSpec((1, D_MODEL, block), lambda l, j: (l, 0, j)),
                  pl.BlockSpec((1, 1, block), lambda l, j: (l, 0, j))],
        out_specs=pl.BlockSpec((1, 8, block), lambda l, j: (l, 0, j)),
        out_shape=jax.ShapeDtypeStruct((DEPTH, 8, width), F32),
        compiler_params=_params(),
        name="adaln",
    )(cond, w_ada, b_ada.reshape(DEPTH, 1, width))


def _inproj_kernel(x_ref, mod_ref, cos_ref, sin_ref, w_ref, kdf_ref, kdb_ref, cdb_ref, kfin_ref,
                   sb0_ref,
                   q_ref, k_ref, kf_ref, v_ref, sg_ref, u_ref, sbc_ref, sbfin_ref, sffin_ref,
                   kb_scr, kfin_scr, sb_scr, sf_scr, *, tm, k_scale, want_final_fwd):
    t = pl.program_id(1)
    nt = pl.num_programs(1)
    cpt = tm // CHUNK

    @pl.when(t == 0)
    def _():
        sb_scr[...] = sb0_ref[0]
        sf_scr[...] = jnp.zeros_like(sf_scr)

    shift = mod_ref[0, 0:1, :]
    scale = mod_ref[0, 1:2, :]
    h = (x_ref[0] * (1.0 + scale) + shift).astype(BF16)

    def proj(col):
        return jnp.dot(h, w_ref[:, col:col + RET_WIDTH], preferred_element_type=F32)

    cos = cos_ref[...]
    sin = sin_ref[...]
    even = (lax.broadcasted_iota(jnp.int32, (tm, LANES), 1) % 2) == 0

    def rope(p):
        swapped = jnp.where(even, pltpu.roll(p, LANES - 1, 1), pltpu.roll(p, 1, 1))
        return p * cos + swapped * sin

    pq = proj(0)
    pk = proj(RET_WIDTH)
    for j in range(N_PAIRS):
        lanes = slice(j * LANES, (j + 1) * LANES)
        q_ref[0, :, lanes] = rope(pq[:, lanes]).astype(BF16)
        kk = rope(pk[:, lanes]) * k_scale
        k_ref[0, :, lanes] = kk.astype(BF16)
        kf_ref[0, :, lanes] = (kk * kdf_ref[:, lanes]).astype(BF16)
        kb_scr[:, lanes] = (kk * kdb_ref[:, lanes]).astype(BF16)
        if want_final_fwd:
            kfin_scr[:, lanes] = (kk * kfin_ref[:, lanes]).astype(BF16)

    v_ref[0] = proj(2 * RET_WIDTH).astype(BF16)
    g = proj(3 * RET_WIDTH)
    sg_ref[0] = (g * jax.nn.sigmoid(g)).astype(BF16)
    a = proj(4 * RET_WIDTH)
    gate = proj(5 * RET_WIDTH)
    u_ref[0] = (a * jax.nn.sigmoid(gate)).astype(BF16)

    bd = _block_diag_mask()
    contract_rows = (((0,), (0,)), ((), ()))
    for c in reversed(range(cpt)):
        rows = slice(c * CHUNK, (c + 1) * CHUNK)
        for j in range(N_PAIRS):
            lanes = slice(j * LANES, (j + 1) * LANES)
            vp = v_ref[0, rows, lanes]
            s = sb_scr[j]
            sbc_ref[0, c, j] = s.astype(BF16)
            kv = lax.dot_general(kb_scr[rows, lanes], vp, contract_rows, preferred_element_type=F32)
            sb_scr[j] = s * cdb_ref[:, lanes] + kv * bd
            if want_final_fwd:
                kvf = lax.dot_general(kfin_scr[rows, lanes], vp, contract_rows,
                                      preferred_element_type=F32)
                sf_scr[j] = sf_scr[j] + kvf * bd

    @pl.when(t == nt - 1)
    def _():
        sbfin_ref[0] = sb_scr[...]
        sffin_ref[0] = sf_scr[...]


def _inproj(x, mod, cos_t, sin_t, w_in, tabs, kfin, sb0, *, tm, k_scale):
    b, n, _ = x.shape
    nt = n // tm
    cpt = tm // CHUNK
    rev = lambda bb, t: (bb, nt - 1 - t, 0)
    rev_tab = lambda bb, t: (nt - 1 - t, 0)
    seq = jax.ShapeDtypeStruct((b, n, RET_WIDTH), BF16)
    state = jax.ShapeDtypeStruct((b, N_PAIRS, LANES, LANES), F32)
    seq_spec = pl.BlockSpec((1, tm, RET_WIDTH), rev)
    state_spec = pl.BlockSpec((1, N_PAIRS, LANES, LANES), lambda bb, t: (bb, 0, 0, 0))
    want_final_fwd = kfin is not None
    if want_final_fwd:
        kfin_spec = pl.BlockSpec((tm, RET_WIDTH), rev_tab)
        kfin_rows = tm
    else:
        kfin = jnp.zeros((8, RET_WIDTH), F32)
        kfin_spec = _resident((8, RET_WIDTH))
        kfin_rows = 16
    return pl.pallas_call(
        functools.partial(_inproj_kernel, tm=tm, k_scale=k_scale, want_final_fwd=want_final_fwd),
        grid=(b, nt),
        in_specs=[pl.BlockSpec((1, tm, D_MODEL), rev),
                  pl.BlockSpec((1, 8, D_MODEL), lambda bb, t: (bb, 0, 0)),
                  pl.BlockSpec((tm, LANES), rev_tab),
                  pl.BlockSpec((tm, LANES), rev_tab),
                  _resident((D_MODEL, IN_WIDTH)),
                  _resident((tm, RET_WIDTH)),
                  _resident((tm, RET_WIDTH)),
                  _resident((1, RET_WIDTH)),
                  kfin_spec,
                  state_spec],
        out_specs=[seq_spec, seq_spec, seq_spec, seq_spec, seq_spec, seq_spec,
                   pl.BlockSpec((1, cpt, N_PAIRS, LANES, LANES),
                                lambda bb, t: (bb, nt - 1 - t, 0, 0, 0)),
                   state_spec, state_spec],
        out_shape=[seq, seq, seq, seq, seq, seq,
                   jax.ShapeDtypeStruct((b, n // CHUNK, N_PAIRS, LANES, LANES), BF16),
                   state, state],
        scratch_shapes=[pltpu.VMEM((tm, RET_WIDTH), BF16),
                        pltpu.VMEM((kfin_rows, RET_WIDTH), BF16),
                        pltpu.VMEM((N_PAIRS, LANES, LANES), F32),
                        pltpu.VMEM((N_PAIRS, LANES, LANES), F32)],
        compiler_params=_params(),
        name="inproj",
    )(x, mod, cos_t, sin_t, w_in, tabs["kdf"], tabs["kdb"], tabs["cdb"], kfin, sb0)


def _group_mean(x, ones_bd):
    hi = x.astype(BF16)
    lo = (x - hi.astype(F32)).astype(BF16)
    return jnp.dot(jnp.concatenate([hi, lo], axis=1), ones_bd, preferred_element_type=F32)


def _mixer_kernel(q_ref, k_ref, kf_ref, v_ref, sg_ref, u_ref, up_ref, un_ref, sbc_ref, sf0_ref,
                  x_ref, mod_ref, mask_ref, rf_ref, rb_ref, cdf_ref, gn_ref, cw_ref, cv_ref,
                  wout_ref, ln_ref,
                  o_ref, sffin_ref,
                  sf_scr, uext_scr, mix_scr, *, tm):
    t = pl.program_id(1)
    nt = pl.num_programs(1)
    cpt = tm // CHUNK

    @pl.when(t == 0)
    def _():
        sf_scr[...] = sf0_ref[0]

    bd = _block_diag_mask()
    r2 = lax.broadcasted_iota(jnp.int32, (2 * LANES, LANES), 0)
    c2 = lax.broadcasted_iota(jnp.int32, (2 * LANES, LANES), 1)
    ones_bd = jnp.where(((r2 % LANES) < HEAD_DIM) == (c2 < HEAD_DIM), 1.0 / HEAD_DIM, 0.0).astype(BF16)
    contract_rows = (((0,), (0,)), ((), ()))
    contract_lanes = (((1,), (1,)), ((), ()))
    for c in range(cpt):
        rows = slice(c * CHUNK, (c + 1) * CHUNK)
        for j in range(N_PAIRS):
            lanes = slice(j * LANES, (j + 1) * LANES)
            qp = q_ref[0, rows, lanes]
            kp = k_ref[0, rows, lanes]
            vp = v_ref[0, rows, lanes]
            scores = lax.dot_general(qp, _stack_block_diag(kp), contract_lanes,
                                     preferred_element_type=F32)
            probs = (scores * mask_ref[j]).astype(BF16)
            sf = sf_scr[j]
            o = jnp.dot(probs, _stack_block_diag(vp), preferred_element_type=F32)
            o = o + rf_ref[:, lanes] * jnp.dot(qp, sf.astype(BF16), preferred_element_type=F32)
            o = o + rb_ref[:, lanes] * jnp.dot(qp, sbc_ref[0, c, j], preferred_element_type=F32)
            kv = lax.dot_general(kf_ref[0, rows, lanes], vp, contract_rows,
                                 preferred_element_type=F32)
            sf_scr[j] = sf * cdf_ref[:, lanes] + kv * bd
            mu = _group_mean(o, ones_bd)
            d = o - mu
            var = _group_mean(d * d, ones_bd)
            on = d * lax.rsqrt(var + LN_EPS) * gn_ref[0:1, lanes] + gn_ref[1:2, lanes]
            mix_scr[rows, lanes] = (sg_ref[0, rows, lanes].astype(F32) * on).astype(BF16)

    @pl.when(t == nt - 1)
    def _():
        sffin_ref[0] = sf_scr[...]

    uext_scr[0:CONV_HALO, :] = jnp.where(t > 0, up_ref[0].astype(F32), 0.0)
    uext_scr[CONV_HALO:CONV_HALO + tm, :] = u_ref[0].astype(F32)
    uext_scr[CONV_HALO + tm:, :] = jnp.where(t < nt - 1, un_ref[0].astype(F32), 0.0)
    first = CONV_HALO - CONV_WIDTH // 2

    def conv_rows(i, carry):
        base = pl.multiple_of(i * CONV_ROWS, CONV_ROWS)
        cols = []
        for j in range(CONV_CH // LANES):
            lanes = slice(j * LANES, (j + 1) * LANES)
            win = uext_scr[pl.ds(base, CONV_ROWS + 2 * CONV_HALO), lanes]
            acc = jnp.broadcast_to(cv_ref[0:1, lanes], (CONV_ROWS, LANES))
            for r in range(SUBLANES):
                shifted = win[r:r + CONV_ROWS + 2 * CONV_HALO - SUBLANES]
                for off in range(r, first + CONV_WIDTH, SUBLANES):
                    if off >= first:
                        tap = off - first
                        a = off - r
                        acc = acc + cw_ref[tap:tap + 1, lanes] * shifted[a:a + CONV_ROWS]
            cols.append(acc)
        y = _layer_norm(jnp.concatenate(cols, axis=1), cv_ref[1:2, :], cv_ref[2:3, :])
        mix_scr[pl.ds(base, CONV_ROWS), RET_WIDTH:] = (y * jax.nn.sigmoid(y)).astype(BF16)
        return carry

    lax.fori_loop(0, tm // CONV_ROWS, conv_rows, 0)

    mix = jnp.dot(mix_scr[...], wout_ref[...], preferred_element_type=F32)
    gate = mod_ref[0, 2:3, :]
    o_ref[0] = _layer_norm(ALPHA * x_ref[0] + gate * mix, ln_ref[0:1, :], ln_ref[1:2, :])


def _mixer(q, k, kf, v, sg, u, sbc, sf0, x, mod, tabs, gn, cw, cv, w_out, ln1, *, tm):
    b, n, _ = x.shape
    nt = n // tm
    cpt = tm // CHUNK
    hpt = tm // CONV_HALO
    last_halo = n // CONV_HALO - 1
    tile = lambda bb, t: (bb, t, 0)
    seq_spec = pl.BlockSpec((1, tm, RET_WIDTH), tile)
    state_spec = pl.BlockSpec((1, N_PAIRS, LANES, LANES), lambda bb, t: (bb, 0, 0, 0))
    return pl.pallas_call(
        functools.partial(_mixer_kernel, tm=tm),
        grid=(b, nt),
        in_specs=[seq_spec, seq_spec, seq_spec, seq_spec, seq_spec, seq_spec,
                  pl.BlockSpec((1, CONV_HALO, CONV_CH),
                               lambda bb, t: (bb, jnp.maximum(t * hpt - 1, 0), 0)),
                  pl.BlockSpec((1, CONV_HALO, CONV_CH),
                               lambda bb, t: (bb, jnp.minimum((t + 1) * hpt, last_halo), 0)),
                  pl.BlockSpec((1, cpt, N_PAIRS, LANES, LANES), lambda bb, t: (bb, t, 0, 0, 0)),
                  state_spec,
                  pl.BlockSpec((1, tm, D_MODEL), tile),
                  pl.BlockSpec((1, 8, D_MODEL), lambda bb, t: (bb, 0, 0)),
                  _resident((N_PAIRS, CHUNK, 2 * CHUNK)),
                  _resident((CHUNK, RET_WIDTH)),
                  _resident((CHUNK, RET_WIDTH)),
                  _resident((1, RET_WIDTH)),
                  _resident((2, RET_WIDTH)),
                  _resident((32, CONV_CH)),
                  _resident((8, CONV_CH)),
                  _resident((D_MODEL, D_MODEL)),
                  _resident((2, D_MODEL))],
        out_specs=[pl.BlockSpec((1, tm, D_MODEL), tile), state_spec],
        out_shape=[jax.ShapeDtypeStruct((b, n, D_MODEL), F32),
                   jax.ShapeDtypeStruct((b, N_PAIRS, LANES, LANES), F32)],
        scratch_shapes=[pltpu.VMEM((N_PAIRS, LANES, LANES), F32),
                        pltpu.VMEM((tm + 2 * CONV_HALO, CONV_CH), F32),
                        pltpu.VMEM((tm, 2 * RET_WIDTH), BF16)],
        compiler_params=_params(),
        name="mixer",
    )(q, k, kf, v, sg, u, u, u, sbc, sf0, x, mod, tabs["mask"], tabs["rf"], tabs["rb"], tabs["cdf"],
      gn, cw, cv, w_out, ln1)


def _mlp_kernel(x_ref, mod_ref, w1_ref, w2_ref, ln_ref, o_ref):
    x = x_ref[0]
    h = (x * (1.0 + mod_ref[0, 4:5, :]) + mod_ref[0, 3:4, :]).astype(BF16)
    y = jnp.zeros(x.shape, F32)
    for j in range(D_FF // FF_BLOCK):
        cols = slice(j * FF_BLOCK, (j + 1) * FF_BLOCK)
        a = jnp.maximum(jnp.dot(h, w1_ref[:, cols], preferred_element_type=F32), 0.0)
        y = y + jnp.dot((a * a).astype(BF16), w2_ref[cols, :], preferred_element_type=F32)
    o_ref[0] = _layer_norm(ALPHA * x + mod_ref[0, 5:6, :] * y, ln_ref[0:1, :], ln_ref[1:2, :])


def _mlp(x, mod, w1, w2, ln2, *, tm):
    b, n, _ = x.shape
    tile = lambda bb, t: (bb, t, 0)
    return pl.pallas_call(
        _mlp_kernel,
        grid=(b, n // tm),
        in_specs=[pl.BlockSpec((1, tm, D_MODEL), tile),
                  pl.BlockSpec((1, 8, D_MODEL), lambda bb, t: (bb, 0, 0)),
                  _resident((D_MODEL, D_FF)),
                  _resident((D_FF, D_MODEL)),
                  _resident((2, D_MODEL))],
        out_specs=pl.BlockSpec((1, tm, D_MODEL), tile),
        out_shape=jax.ShapeDtypeStruct((b, n, D_MODEL), F32),
        compiler_params=_params(),
        name="mlp",
    )(x, mod, w1, w2, ln2)


def _rope_tables(n):
    rows = n // GRID_W
    row = jnp.repeat(jnp.arange(rows, dtype=F32), GRID_W)
    col = jnp.tile(jnp.arange(GRID_W, dtype=F32), rows)
    n_freq = HEAD_DIM // 4
    inv = ROPE_BASE ** (-jnp.arange(n_freq, dtype=F32) / n_freq)
    ang = jnp.concatenate([row[:, None] * inv, col[:, None] * inv], axis=-1)
    cos = jnp.repeat(jnp.cos(ang), 2, axis=-1)
    sin = jnp.repeat(jnp.sin(ang), 2, axis=-1) * jnp.tile(jnp.array([-1.0, 1.0], F32), HEAD_DIM // 2)
    return jnp.tile(cos, (1, 2)), jnp.tile(sin, (1, 2))


def _per_lane(t):
    return jnp.repeat(t.T, HEAD_DIM, axis=1)


def _decay_tables(lg_f, lg_b, tm):
    idx = jnp.arange(CHUNK, dtype=F32)
    lf = lg_f.astype(F32)[:, None]
    lb = lg_b.astype(F32)[:, None]
    diff = idx[:, None] - idx[None, :]
    fwd = jnp.where(diff >= 0, jnp.exp(lf[:, :, None] * jnp.maximum(diff, 0.0)), 0.0)
    bwd = jnp.where(diff <= 0, jnp.exp(lb[:, :, None] * jnp.maximum(-diff, 0.0)), 0.0)
    mask = (fwd + bwd).reshape(N_PAIRS, 2, CHUNK, CHUNK).transpose(0, 2, 1, 3)
    reps = tm // CHUNK
    return {
        "mask": mask.reshape(N_PAIRS, CHUNK, 2 * CHUNK),
        "rf": _per_lane(jnp.exp(lf * (idx + 1.0))),
        "rb": _per_lane(jnp.exp(lb * (CHUNK - idx))),
        "kdf": jnp.tile(_per_lane(jnp.exp(lf * (CHUNK - 1.0 - idx))), (reps, 1)),
        "kdb": jnp.tile(_per_lane(jnp.exp(lb * idx)), (reps, 1)),
        "cdf": _per_lane(jnp.exp(lf * CHUNK)),
        "cdb": _per_lane(jnp.exp(lb * CHUNK)),
    }


def _final_weights(lg_f, n):
    pos = jnp.arange(n, dtype=F32)
    return _per_lane(jnp.exp(lg_f.astype(F32)[:, None] * (n - 1.0 - pos)))


def _rows8(*rows):
    out = jnp.stack(rows, axis=-2)
    pad = [(0, 0)] * (out.ndim - 2) + [(0, 8 - len(rows)), (0, 0)]
    return jnp.pad(out, pad)


def kernel(x, c, ctx, c_ctx, w_ada, b_ada, w_in, ret_log_rate_fwd, ret_log_rate_bwd, ret_gn_w, ret_gn_b,
           conv_w, conv_b, conv_ln_w, conv_ln_b, w_out, ln1_w, ln1_b, w_ff1, w_ff2, ln2_w, ln2_b):
    b, n, _ = x.shape
    n_ctx = ctx.shape[1]
    tm_lat, tm_ctx, tm_mlp = 512, n_ctx, 512

    cond = jnp.zeros((8, D_MODEL), F32).at[:b].set(c).at[b].set(c_ctx)
    ada = _adaln(cond, w_ada, b_ada).reshape(DEPTH, 8, 6, D_MODEL)

    cos_lat, sin_lat = _rope_tables(n)
    cos_ctx = jnp.ones((n_ctx, LANES), F32)
    sin_ctx = jnp.zeros((n_ctx, LANES), F32)
    zero_state = jnp.zeros((b, N_PAIRS, LANES, LANES), F32)

    for l in range(DEPTH):
        last = l == DEPTH - 1
        mod_lat = jnp.pad(ada[l, :b], ((0, 0), (0, 2), (0, 0)))
        mod_ctx = jnp.broadcast_to(jnp.pad(ada[l, b], ((0, 2), (0, 0))), (b, 8, D_MODEL))
        lg_f = -jnp.exp(ret_log_rate_fwd[l])
        lg_b = -jnp.exp(ret_log_rate_bwd[l])
        tabs_lat = _decay_tables(lg_f, lg_b, tm_lat)
        tabs_ctx = _decay_tables(lg_f, lg_b, tm_ctx)
        w_in_l = w_in[l].astype(BF16)
        w_out_l = w_out[l].astype(BF16)
        w1_l = w_ff1[l].astype(BF16)
        w2_l = w_ff2[l].astype(BF16)
        gn = jnp.stack([ret_gn_w[l], ret_gn_b[l]])
        cw = jnp.pad(conv_w[l], ((0, 32 - CONV_WIDTH), (0, 0)))
        cv = _rows8(conv_b[l], conv_ln_w[l], conv_ln_b[l])
        ln1 = jnp.stack([ln1_w[l], ln1_b[l]])
        ln2 = jnp.stack([ln2_w[l], ln2_b[l]])

        k_scale_ctx = 1.0 if last else HEAD_DIM ** -0.5
        (qc, kc, kfc, vc, sgc, uc, sbc_c, sb_fin, sf_fin) = _inproj(
            ctx, mod_ctx, cos_ctx, sin_ctx, w_in_l, tabs_ctx, _final_weights(lg_f, n_ctx), zero_state,
            tm=tm_ctx, k_scale=k_scale_ctx)

        (q, k, kf, v, sg, u, sbc, _, _) = _inproj(
            x, mod_lat, cos_lat, sin_lat, w_in_l, tabs_lat, None, sb_fin,
            tm=tm_lat, k_scale=HEAD_DIM ** -0.5)
        x, _ = _mixer(q, k, kf, v, sg, u, sbc, sf_fin, x, mod_lat, tabs_lat, gn, cw, cv, w_out_l, ln1,
                      tm=tm_lat)
        x = _mlp(x, mod_lat, w1_l, w2_l, ln2, tm=tm_mlp)

        if not last:
            ctx, _ = _mixer(qc, kc, kfc, vc, sgc, uc, sbc_c, zero_state, ctx, mod_ctx, tabs_ctx, gn, cw,
                            cv, w_out_l, ln1, tm=tm_ctx)
            ctx = _mlp(ctx, mod_ctx, w1_l, w2_l, ln2, tm=tm_ctx)

    return x
```

```python
import functools

import jax
import jax.numpy as jnp
from jax import lax
from jax.experimental import pallas as pl
from jax.experimental.pallas import tpu as pltpu

D_MODEL = 1024
DEPTH = 4
GRID_W = 64
RET_HEADS = 8
HEAD_DIM = 64
RET_WIDTH = RET_HEADS * HEAD_DIM
CONV_CH = 512
CONV_WIDTH = 31
CONV_HALO = 16
IN_WIDTH = 3072
D_FF = 4 * D_MODEL
CHUNK = 128
ROPE_BASE = 10000.0
LN_EPS = 1e-5
ALPHA = (2 * DEPTH) ** 0.25
LANES = 128
N_PAIRS = RET_HEADS // 2
FF_BLOCK = 1024
SUBLANES = 8
SHIFT_K = 256
SHIFT_ROWS = CHUNK + 2 * CONV_HALO - SUBLANES
VMEM_LIMIT = 56 * 1024 * 1024

F32 = jnp.float32
BF16 = jnp.bfloat16


def _resident(shape):
    zeros = (0,) * len(shape)
    return pl.BlockSpec(shape, lambda *_: zeros, pipeline_mode=pl.Buffered(1))


def _params():
    return pltpu.CompilerParams(dimension_semantics=("arbitrary", "arbitrary"),
                                vmem_limit_bytes=VMEM_LIMIT)


def _layer_norm(x, w, b):
    mu = jnp.mean(x, axis=-1, keepdims=True)
    d = x - mu
    var = jnp.mean(d * d, axis=-1, keepdims=True)
    return d * lax.rsqrt(var + LN_EPS) * w + b


def _pair_masks(rows):
    lane = lax.broadcasted_iota(jnp.int32, (rows, LANES), 1)
    return lane < HEAD_DIM, lane >= HEAD_DIM


def _block_diag_mask():
    r = lax.broadcasted_iota(jnp.int32, (LANES, LANES), 0)
    c = lax.broadcasted_iota(jnp.int32, (LANES, LANES), 1)
    return ((r < HEAD_DIM) == (c < HEAD_DIM)).astype(F32)


def _stack_block_diag(t):
    left, right = _pair_masks(t.shape[0])
    zero = jnp.zeros_like(t)
    return jnp.concatenate([jnp.where(left, t, zero), jnp.where(right, t, zero)], axis=0)


def _adaln_kernel(cond_ref, w_ref, b_ref, o_ref):
    cond = cond_ref[...]
    act = cond * jax.nn.sigmoid(cond)
    o_ref[0] = jnp.dot(act, w_ref[0], preferred_element_type=F32,
                       precision=lax.Precision.HIGHEST) + b_ref[0]


def _adaln(cond, w_ada, b_ada):
    width = w_ada.shape[-1]
    block = 1536
    return pl.pallas_call(
        _adaln_kernel,
        grid=(DEPTH, width // block),
        in_specs=[pl.BlockSpec((8, D_MODEL), lambda l, j: (0, 0)),
                  pl.BlockSpec((1, D_MODEL, block), lambda l, j: (l, 0, j)),
                  pl.BlockSpec((1, 1, block), lambda l, j: (l, 0, j))],
        out_specs=pl.BlockSpec((1, 8, block), lambda l, j: (l, 0, j)),
        out_shape=jax.ShapeDtypeStruct((DEPTH, 8, width), F32),
        compiler_params=_params(),
        name="adaln",
    )(cond, w_ada, b_ada.reshape(DEPTH, 1, width))


def _inproj_kernel(x_ref, mod_ref, cos_ref, sin_ref, w_ref, rf_ref, rb_ref, kdf_ref, kdb_ref, cdb_ref,
                   kfin_ref, sb0_ref,
                   q_ref, qf_ref, qb_ref, k_ref, kf_ref, v_ref, sg_ref, u_ref, sbc_ref, sbfin_ref, sffin_ref,
                   kb_scr, kfin_scr, sb_scr, sf_scr, *, tm, k_scale, want_final_fwd):
    t = pl.program_id(1)
    nt = pl.num_programs(1)
    cpt = tm // CHUNK

    @pl.when(t == 0)
    def _():
        sb_scr[...] = sb0_ref[0]
        sf_scr[...] = jnp.zeros_like(sf_scr)

    shift = mod_ref[0, 0:1, :]
    scale = mod_ref[0, 1:2, :]
    h = (x_ref[0] * (1.0 + scale) + shift).astype(BF16)

    def proj(col):
        return jnp.dot(h, w_ref[:, col:col + RET_WIDTH], preferred_element_type=F32)

    cos = cos_ref[...]
    sin = sin_ref[...]
    even = (lax.broadcasted_iota(jnp.int32, (tm, LANES), 1) % 2) == 0

    def rope(p):
        swapped = jnp.where(even, pltpu.roll(p, LANES - 1, 1), pltpu.roll(p, 1, 1))
        return p * cos + swapped * sin

    pq = proj(0)
    pk = proj(RET_WIDTH)
    for j in range(N_PAIRS):
        lanes = slice(j * LANES, (j + 1) * LANES)
        qq = rope(pq[:, lanes])
        q_ref[0, :, lanes] = qq.astype(BF16)
        qf_ref[0, :, lanes] = (qq * rf_ref[:, lanes]).astype(BF16)
        qb_ref[0, :, lanes] = (qq * rb_ref[:, lanes]).astype(BF16)
        kk = rope(pk[:, lanes]) * k_scale
        k_ref[0, :, lanes] = kk.astype(BF16)
        kf_ref[0, :, lanes] = (kk * kdf_ref[:, lanes]).astype(BF16)
        kb_scr[:, lanes] = (kk * kdb_ref[:, lanes]).astype(BF16)
        if want_final_fwd:
            kfin_scr[:, lanes] = (kk * kfin_ref[:, lanes]).astype(BF16)

    v_ref[0] = proj(2 * RET_WIDTH).astype(BF16)
    g = proj(3 * RET_WIDTH)
    sg_ref[0] = (g * jax.nn.sigmoid(g)).astype(BF16)
    a = proj(4 * RET_WIDTH)
    gate = proj(5 * RET_WIDTH)
    u_ref[0] = (a * jax.nn.sigmoid(gate)).astype(BF16)

    bd = _block_diag_mask()
    contract_rows = (((0,), (0,)), ((), ()))
    for c in reversed(range(cpt)):
        rows = slice(c * CHUNK, (c + 1) * CHUNK)
        for j in range(N_PAIRS):
            lanes = slice(j * LANES, (j + 1) * LANES)
            vp = v_ref[0, rows, lanes]
            s = sb_scr[j]
            sbc_ref[0, c, j] = s.astype(BF16)
            kv = lax.dot_general(kb_scr[rows, lanes], vp, contract_rows, preferred_element_type=F32)
            sb_scr[j] = s * cdb_ref[:, lanes] + kv * bd
            if want_final_fwd:
                kvf = lax.dot_general(kfin_scr[rows, lanes], vp, contract_rows,
                                      preferred_element_type=F32)
                sf_scr[j] = sf_scr[j] + kvf * bd

    @pl.when(t == nt - 1)
    def _():
        sbfin_ref[0] = sb_scr[...]
        sffin_ref[0] = sf_scr[...]


def _inproj(x, mod, cos_t, sin_t, w_in, tabs, kfin, sb0, *, tm, k_scale):
    b, n, _ = x.shape
    nt = n // tm
    cpt = tm // CHUNK
    rev = lambda bb, t: (bb, nt - 1 - t, 0)
    rev_tab = lambda bb, t: (nt - 1 - t, 0)
    seq = jax.ShapeDtypeStruct((b, n, RET_WIDTH), BF16)
    state = jax.ShapeDtypeStruct((b, N_PAIRS, LANES, LANES), F32)
    seq_spec = pl.BlockSpec((1, tm, RET_WIDTH), rev)
    state_spec = pl.BlockSpec((1, N_PAIRS, LANES, LANES), lambda bb, t: (bb, 0, 0, 0))
    want_final_fwd = kfin is not None
    if want_final_fwd:
        kfin_spec = pl.BlockSpec((tm, RET_WIDTH), rev_tab)
        kfin_rows = tm
    else:
        kfin = jnp.zeros((8, RET_WIDTH), F32)
        kfin_spec = _resident((8, RET_WIDTH))
        kfin_rows = 16
    return pl.pallas_call(
        functools.partial(_inproj_kernel, tm=tm, k_scale=k_scale, want_final_fwd=want_final_fwd),
        grid=(b, nt),
        in_specs=[pl.BlockSpec((1, tm, D_MODEL), rev),
                  pl.BlockSpec((1, 8, D_MODEL), lambda bb, t: (bb, 0, 0)),
                  pl.BlockSpec((tm, LANES), rev_tab),
                  pl.BlockSpec((tm, LANES), rev_tab),
                  _resident((D_MODEL, IN_WIDTH)),
                  _resident((tm, RET_WIDTH)),
                  _resident((tm, RET_WIDTH)),
                  _resident((tm, RET_WIDTH)),
                  _resident((tm, RET_WIDTH)),
                  _resident((1, RET_WIDTH)),
                  kfin_spec,
                  state_spec],
        out_specs=[seq_spec] * 8 + [
                   pl.BlockSpec((1, cpt, N_PAIRS, LANES, LANES),
                                lambda bb, t: (bb, nt - 1 - t, 0, 0, 0)),
                   state_spec, state_spec],
        out_shape=[seq] * 8 + [
                   jax.ShapeDtypeStruct((b, n // CHUNK, N_PAIRS, LANES, LANES), BF16),
                   state, state],
        scratch_shapes=[pltpu.VMEM((tm, RET_WIDTH), BF16),
                        pltpu.VMEM((kfin_rows, RET_WIDTH), BF16),
                        pltpu.VMEM((N_PAIRS, LANES, LANES), F32),
                        pltpu.VMEM((N_PAIRS, LANES, LANES), F32)],
        compiler_params=_params(),
        name="inproj",
    )(x, mod, cos_t, sin_t, w_in, tabs["rf"], tabs["rb"], tabs["kdf"], tabs["kdb"], tabs["cdb"], kfin, sb0)


def _mixer_kernel(q_ref, qf_ref, qb_ref, k_ref, kf_ref, v_ref, sg_ref, u_ref, up_ref, un_ref, sbc_ref,
                  sf0_ref, x_ref, mod_ref, mask_ref, cdf_ref, gn_ref, cw_ref, cv_ref, shift_ref, ones_ref,
                  wout_ref, ln_ref,
                  o_ref, sffin_ref,
                  sf_scr, uext_scr, mix_scr, *, tm):
    t = pl.program_id(1)
    nt = pl.num_programs(1)
    cpt = tm // CHUNK
    lane_of = [slice(j * LANES, (j + 1) * LANES) for j in range(N_PAIRS)]
    rows_of = [slice(c * CHUNK, (c + 1) * CHUNK) for c in range(cpt)]

    @pl.when(t == 0)
    def _():
        sf_scr[...] = sf0_ref[0]

    uext_scr[0:CONV_HALO, :] = jnp.where(t > 0, up_ref[0], jnp.zeros_like(up_ref[0]))
    uext_scr[CONV_HALO:CONV_HALO + tm, :] = u_ref[0]
    uext_scr[CONV_HALO + tm:2 * CONV_HALO + tm, :] = jnp.where(t < nt - 1, un_ref[0],
                                                               jnp.zeros_like(un_ref[0]))
    uext_scr[2 * CONV_HALO + tm:, :] = jnp.zeros((SHIFT_K - 2 * CONV_HALO, CONV_CH), BF16)

    def conv_chunk(c):
        win = uext_scr[c * CHUNK:c * CHUNK + SHIFT_K, :]
        shifted = jnp.dot(shift_ref[...], win, preferred_element_type=F32)
        aligned = win[0:CHUNK + 2 * CONV_HALO].astype(F32)
        first = CONV_HALO - CONV_WIDTH // 2
        cols = []
        for lanes in lane_of:
            acc = jnp.broadcast_to(cv_ref[0:1, lanes], (CHUNK, LANES))
            for off in range(first, first + CONV_WIDTH):
                r, a = off % SUBLANES, off - off % SUBLANES
                if r == 0:
                    src = aligned[a:a + CHUNK, lanes]
                else:
                    src = shifted[(r - 1) * SHIFT_ROWS + a:(r - 1) * SHIFT_ROWS + a + CHUNK, lanes]
                acc = acc + cw_ref[off - first:off - first + 1, lanes] * src
            cols.append(acc)
        y = _layer_norm(jnp.concatenate(cols, axis=1), cv_ref[1:2, :], cv_ref[2:3, :])
        mix_scr[rows_of[c], RET_WIDTH:] = (y * jax.nn.sigmoid(y)).astype(BF16)

    bd = _block_diag_mask()
    contract_rows = (((0,), (0,)), ((), ()))
    contract_lanes = (((1,), (1,)), ((), ()))
    kvs = [[lax.dot_general(kf_ref[0, rows_of[c], lane_of[j]], v_ref[0, rows_of[c], lane_of[j]],
                            contract_rows, preferred_element_type=F32)
            for c in range(cpt)] for j in range(N_PAIRS)]
    scores = [[lax.dot_general(q_ref[0, rows_of[c], lane_of[j]],
                               _stack_block_diag(k_ref[0, rows_of[c], lane_of[j]]),
                               contract_lanes, preferred_element_type=F32)
               for c in range(cpt)] for j in range(N_PAIRS)]
    conv_chunk(0)
    outs = []
    for j in range(N_PAIRS):
        lanes = lane_of[j]
        s = sf_scr[j]
        per_chunk = []
        for c in range(cpt):
            rows = rows_of[c]
            probs = (scores[j][c] * mask_ref[j]).astype(BF16)
            lhs = jnp.concatenate([probs, qf_ref[0, rows, lanes], qb_ref[0, rows, lanes]], axis=1)
            rhs = jnp.concatenate([_stack_block_diag(v_ref[0, rows, lanes]), s.astype(BF16),
                                   sbc_ref[0, c, j]], axis=0)
            per_chunk.append(jnp.dot(lhs, rhs, preferred_element_type=F32))
            s = s * cdf_ref[:, lanes] + kvs[j][c] * bd
        sf_scr[j] = s
        outs.append(jnp.concatenate(per_chunk, axis=0))
    if cpt > 1:
        conv_chunk(1)

    @pl.when(t == nt - 1)
    def _():
        sffin_ref[0] = sf_scr[...]

    halves = [jnp.concatenate(outs[0:2], axis=1), jnp.concatenate(outs[2:4], axis=1)]
    means = [jnp.dot(h.astype(BF16), ones_ref[...], preferred_element_type=F32) for h in halves]
    cent = [h - m for h, m in zip(halves, means)]
    for c in range(2, cpt - 1):
        conv_chunk(c)
    var = [jnp.dot((d * d).astype(BF16), ones_ref[...], preferred_element_type=F32) for d in cent]
    for i in range(2):
        lanes = slice(i * 2 * LANES, (i + 1) * 2 * LANES)
        on = cent[i] * lax.rsqrt(var[i] + LN_EPS) * gn_ref[0:1, lanes] + gn_ref[1:2, lanes]
        mix_scr[:, lanes] = (sg_ref[0, :, lanes].astype(F32) * on).astype(BF16)
    if cpt > 2:
        conv_chunk(cpt - 1)

    mix = jnp.dot(mix_scr[:, :RET_WIDTH], wout_ref[:RET_WIDTH, :], preferred_element_type=F32)
    mix = mix + jnp.dot(mix_scr[:, RET_WIDTH:], wout_ref[RET_WIDTH:, :], preferred_element_type=F32)
    gate = mod_ref[0, 2:3, :]
    o_ref[0] = _layer_norm(ALPHA * x_ref[0] + gate * mix, ln_ref[0:1, :], ln_ref[1:2, :])


def _shift_matrix():
    row = jnp.arange((SUBLANES - 1) * SHIFT_ROWS)
    col = jnp.arange(SHIFT_K)
    return (col[None, :] == (row % SHIFT_ROWS + row // SHIFT_ROWS + 1)[:, None]).astype(BF16)


def _group_ones():
    g = jnp.arange(2 * LANES) // HEAD_DIM
    return jnp.where(g[:, None] == g[None, :], 1.0 / HEAD_DIM, 0.0).astype(BF16)


def _mixer(q, qf, qb, k, kf, v, sg, u, sbc, sf0, x, mod, tabs, gn, cw, cv, w_out, ln1, *, tm):
    b, n, _ = x.shape
    nt = n // tm
    cpt = tm // CHUNK
    hpt = tm // CONV_HALO
    last_halo = n // CONV_HALO - 1
    tile = lambda bb, t: (bb, t, 0)
    seq_spec = pl.BlockSpec((1, tm, RET_WIDTH), tile)
    state_spec = pl.BlockSpec((1, N_PAIRS, LANES, LANES), lambda bb, t: (bb, 0, 0, 0))
    shift = _shift_matrix()
    ones = _group_ones()
    return pl.pallas_call(
        functools.partial(_mixer_kernel, tm=tm),
        grid=(b, nt),
        in_specs=[seq_spec, seq_spec, seq_spec, seq_spec, seq_spec, seq_spec, seq_spec, seq_spec,
                  pl.BlockSpec((1, CONV_HALO, CONV_CH),
                               lambda bb, t: (bb, jnp.maximum(t * hpt - 1, 0), 0)),
                  pl.BlockSpec((1, CONV_HALO, CONV_CH),
                               lambda bb, t: (bb, jnp.minimum((t + 1) * hpt, last_halo), 0)),
                  pl.BlockSpec((1, cpt, N_PAIRS, LANES, LANES), lambda bb, t: (bb, t, 0, 0, 0)),
                  state_spec,
                  pl.BlockSpec((1, tm, D_MODEL), tile),
                  pl.BlockSpec((1, 8, D_MODEL), lambda bb, t: (bb, 0, 0)),
                  _resident((N_PAIRS, CHUNK, 2 * CHUNK)),
                  _resident((1, RET_WIDTH)),
                  _resident((2, RET_WIDTH)),
                  _resident((32, CONV_CH)),
                  _resident((8, CONV_CH)),
                  _resident(shift.shape),
                  _resident(ones.shape),
                  _resident((D_MODEL, D_MODEL)),
                  _resident((2, D_MODEL))],
        out_specs=[pl.BlockSpec((1, tm, D_MODEL), tile), state_spec],
        out_shape=[jax.ShapeDtypeStruct((b, n, D_MODEL), F32),
                   jax.ShapeDtypeStruct((b, N_PAIRS, LANES, LANES), F32)],
        scratch_shapes=[pltpu.VMEM((N_PAIRS, LANES, LANES), F32),
                        pltpu.VMEM((tm + SHIFT_K, CONV_CH), BF16),
                        pltpu.VMEM((tm, 2 * RET_WIDTH), BF16)],
        compiler_params=_params(),
        name="mixer",
    )(q, qf, qb, k, kf, v, sg, u, u, u, sbc, sf0, x, mod, tabs["mask"], tabs["cdf"],
      gn, cw, cv, shift, ones, w_out, ln1)


def _mlp_kernel(x_ref, mod_ref, w1_ref, w2_ref, ln_ref, o_ref):
    x = x_ref[0]
    h = (x * (1.0 + mod_ref[0, 4:5, :]) + mod_ref[0, 3:4, :]).astype(BF16)
    y = jnp.zeros(x.shape, F32)
    for j in range(D_FF // FF_BLOCK):
        cols = slice(j * FF_BLOCK, (j + 1) * FF_BLOCK)
        a = jnp.maximum(jnp.dot(h, w1_ref[:, cols], preferred_element_type=F32), 0.0)
        y = y + jnp.dot((a * a).astype(BF16), w2_ref[cols, :], preferred_element_type=F32)
    o_ref[0] = _layer_norm(ALPHA * x + mod_ref[0, 5:6, :] * y, ln_ref[0:1, :], ln_ref[1:2, :])


def _mlp(x, mod, w1, w2, ln2, *, tm):
    b, n, _ = x.shape
    tile = lambda bb, t: (bb, t, 0)
    return pl.pallas_call(
        _mlp_kernel,
        grid=(b, n // tm),
        in_specs=[pl.BlockSpec((1, tm, D_MODEL), tile),
                  pl.BlockSpec((1, 8, D_MODEL), lambda bb, t: (bb, 0, 0)),
                  _resident((D_MODEL, D_FF)),
                  _resident((D_FF, D_MODEL)),
                  _resident((2, D_MODEL))],
        out_specs=pl.BlockSpec((1, tm, D_MODEL), tile),
        out_shape=jax.ShapeDtypeStruct((b, n, D_MODEL), F32),
        compiler_params=_params(),
        name="mlp",
    )(x, mod, w1, w2, ln2)


def _rope_tables(n):
    rows = n // GRID_W
    row = jnp.repeat(jnp.arange(rows, dtype=F32), GRID_W)
    col = jnp.tile(jnp.arange(GRID_W, dtype=F32), rows)
    n_freq = HEAD_DIM // 4
    inv = ROPE_BASE ** (-jnp.arange(n_freq, dtype=F32) / n_freq)
    ang = jnp.concatenate([row[:, None] * inv, col[:, None] * inv], axis=-1)
    cos = jnp.repeat(jnp.cos(ang), 2, axis=-1)
    sin = jnp.repeat(jnp.sin(ang), 2, axis=-1) * jnp.tile(jnp.array([-1.0, 1.0], F32), HEAD_DIM // 2)
    return jnp.tile(cos, (1, 2)), jnp.tile(sin, (1, 2))


def _per_lane(t):
    return jnp.repeat(t.T, HEAD_DIM, axis=1)


def _decay_tables(lg_f, lg_b, tm):
    idx = jnp.arange(CHUNK, dtype=F32)
    lf = lg_f.astype(F32)[:, None]
    lb = lg_b.astype(F32)[:, None]
    diff = idx[:, None] - idx[None, :]
    fwd = jnp.where(diff >= 0, jnp.exp(lf[:, :, None] * jnp.maximum(diff, 0.0)), 0.0)
    bwd = jnp.where(diff <= 0, jnp.exp(lb[:, :, None] * jnp.maximum(-diff, 0.0)), 0.0)
    mask = (fwd + bwd).reshape(N_PAIRS, 2, CHUNK, CHUNK).transpose(0, 2, 1, 3)
    reps = tm // CHUNK
    return {
        "mask": mask.reshape(N_PAIRS, CHUNK, 2 * CHUNK),
        "rf": jnp.tile(_per_lane(jnp.exp(lf * (idx + 1.0))), (reps, 1)),
        "rb": jnp.tile(_per_lane(jnp.exp(lb * (CHUNK - idx))), (reps, 1)),
        "kdf": jnp.tile(_per_lane(jnp.exp(lf * (CHUNK - 1.0 - idx))), (reps, 1)),
        "kdb": jnp.tile(_per_lane(jnp.exp(lb * idx)), (reps, 1)),
        "cdf": _per_lane(jnp.exp(lf * CHUNK)),
        "cdb": _per_lane(jnp.exp(lb * CHUNK)),
    }


def _final_weights(lg_f, n):
    pos = jnp.arange(n, dtype=F32)
    return _per_lane(jnp.exp(lg_f.astype(F32)[:, None] * (n - 1.0 - pos)))


def _rows8(*rows):
    out = jnp.stack(rows, axis=-2)
    pad = [(0, 0)] * (out.ndim - 2) + [(0, 8 - len(rows)), (0, 0)]
    return jnp.pad(out, pad)


def kernel(x, c, ctx, c_ctx, w_ada, b_ada, w_in, ret_log_rate_fwd, ret_log_rate_bwd, ret_gn_w, ret_gn_b,
           conv_w, conv_b, conv_ln_w, conv_ln_b, w_out, ln1_w, ln1_b, w_ff1, w_ff2, ln2_w, ln2_b):
    b, n, _ = x.shape
    n_ctx = ctx.shape[1]
    tm_lat, tm_ctx, tm_mlp = 512, n_ctx, 512

    cond = jnp.zeros((8, D_MODEL), F32).at[:b].set(c).at[b].set(c_ctx)
    ada = _adaln(cond, w_ada, b_ada).reshape(DEPTH, 8, 6, D_MODEL)

    cos_lat, sin_lat = _rope_tables(n)
    cos_ctx = jnp.ones((n_ctx, LANES), F32)
    sin_ctx = jnp.zeros((n_ctx, LANES), F32)
    zero_state = jnp.zeros((b, N_PAIRS, LANES, LANES), F32)

    for l in range(DEPTH):
        last = l == DEPTH - 1
        mod_lat = jnp.pad(ada[l, :b], ((0, 0), (0, 2), (0, 0)))
        mod_ctx = jnp.broadcast_to(jnp.pad(ada[l, b], ((0, 2), (0, 0))), (b, 8, D_MODEL))
        lg_f = -jnp.exp(ret_log_rate_fwd[l])
        lg_b = -jnp.exp(ret_log_rate_bwd[l])
        tabs_lat = _decay_tables(lg_f, lg_b, tm_lat)
        tabs_ctx = _decay_tables(lg_f, lg_b, tm_ctx)
        w_in_l = w_in[l].astype(BF16)
        w_out_l = w_out[l].astype(BF16)
        w1_l = w_ff1[l].astype(BF16)
        w2_l = w_ff2[l].astype(BF16)
        gn = jnp.stack([ret_gn_w[l], ret_gn_b[l]])
        cw = jnp.pad(conv_w[l], ((0, 32 - CONV_WIDTH), (0, 0)))
        cv = _rows8(conv_b[l], conv_ln_w[l], conv_ln_b[l])
        ln1 = jnp.stack([ln1_w[l], ln1_b[l]])
        ln2 = jnp.stack([ln2_w[l], ln2_b[l]])

        k_scale_ctx = 1.0 if last else HEAD_DIM ** -0.5
        (qc, qfc, qbc, kc, kfc, vc, sgc, uc, sbc_c, sb_fin, sf_fin) = _inproj(
            ctx, mod_ctx, cos_ctx, sin_ctx, w_in_l, tabs_ctx, _final_weights(lg_f, n_ctx), zero_state,
            tm=tm_ctx, k_scale=k_scale_ctx)

        (q, qf, qb, k, kf, v, sg, u, sbc, _, _) = _inproj(
            x, mod_lat, cos_lat, sin_lat, w_in_l, tabs_lat, None, sb_fin,
            tm=tm_lat, k_scale=HEAD_DIM ** -0.5)
        x, _ = _mixer(q, qf, qb, k, kf, v, sg, u, sbc, sf_fin, x, mod_lat, tabs_lat, gn, cw, cv, w_out_l, ln1,
                      tm=tm_lat)
        x = _mlp(x, mod_lat, w1_l, w2_l, ln2, tm=tm_mlp)

        if not last:
            ctx, _ = _mixer(qc, qfc, qbc, kc, kfc, vc, sgc, uc, sbc_c, zero_state, ctx, mod_ctx, tabs_ctx, gn, cw,
                            cv, w_out_l, ln1, tm=tm_ctx)
            ctx = _mlp(ctx, mod_ctx, w1_l, w2_l, ln2, tm=tm_ctx)

    return x
```

```python
import functools

import jax
import jax.numpy as jnp
from jax import lax
from jax.experimental import pallas as pl
from jax.experimental.pallas import tpu as pltpu

D_MODEL = 1024
DEPTH = 4
GRID_W = 64
RET_HEADS = 8
HEAD_DIM = 64
RET_WIDTH = RET_HEADS * HEAD_DIM
CONV_CH = 512
CONV_WIDTH = 31
CONV_HALO = 16
IN_WIDTH = 3072
D_FF = 4 * D_MODEL
CHUNK = 128
ROPE_BASE = 10000.0
LN_EPS = 1e-5
ALPHA = (2 * DEPTH) ** 0.25
LANES = 128
N_PAIRS = RET_HEADS // 2
FF_BLOCK = 1024
MLP_ROWS = 256
N_SLABS = CONV_CH // LANES
HALF = CHUNK // 2
VMEM_LIMIT = 56 * 1024 * 1024

F32 = jnp.float32
BF16 = jnp.bfloat16


def _resident(shape):
    zeros = (0,) * len(shape)
    return pl.BlockSpec(shape, lambda *_: zeros, pipeline_mode=pl.Buffered(1))


def _layer_resident(layer, shape):
    zeros = (0,) * len(shape)
    return pl.BlockSpec((None,) + tuple(shape), lambda *_: (layer,) + zeros, pipeline_mode=pl.Buffered(1))


def _params():
    return pltpu.CompilerParams(dimension_semantics=("arbitrary", "arbitrary"),
                                vmem_limit_bytes=VMEM_LIMIT)


def _layer_norm(x, w, b):
    mu = jnp.mean(x, axis=-1, keepdims=True)
    d = x - mu
    var = jnp.mean(d * d, axis=-1, keepdims=True)
    return d * lax.rsqrt(var + LN_EPS) * w + b


def _pair_masks(rows):
    lane = lax.broadcasted_iota(jnp.int32, (rows, LANES), 1)
    return lane < HEAD_DIM, lane >= HEAD_DIM


def _block_diag_mask():
    r = lax.broadcasted_iota(jnp.int32, (LANES, LANES), 0)
    c = lax.broadcasted_iota(jnp.int32, (LANES, LANES), 1)
    return ((r < HEAD_DIM) == (c < HEAD_DIM)).astype(F32)


def _adaln_kernel(cond_ref, w_ref, b_ref, o_ref):
    cond = cond_ref[...]
    act = cond * jax.nn.sigmoid(cond)
    o_ref[0] = jnp.dot(act, w_ref[0], preferred_element_type=F32,
                       precision=lax.Precision.HIGHEST) + b_ref[0]


def _adaln(cond, w_ada, b_ada):
    width = w_ada.shape[-1]
    block = 1536
    return pl.pallas_call(
        _adaln_kernel,
        grid=(DEPTH, width // block),
        in_specs=[pl.BlockSpec((8, D_MODEL), lambda l, j: (0, 0)),
                  pl.BlockSpec((1, D_MODEL, block), lambda l, j: (l, 0, j)),
                  pl.BlockSpec((1, 1, block), lambda l, j: (l, 0, j))],
        out_specs=pl.BlockSpec((1, 8, block), lambda l, j: (l, 0, j)),
        out_shape=jax.ShapeDtypeStruct((DEPTH, 8, width), F32),
        compiler_params=_params(),
        name="adaln",
    )(cond, w_ada, b_ada.reshape(DEPTH, 1, width))


def _inproj_kernel(x_ref, mod_ref, cos_ref, sin_ref, w_ref, rf_ref, rb_ref, kdf_ref, kdb_ref, cdb_ref,
                   kfin_ref, sb0_ref,
                   q_ref, qf_ref, qb_ref, kk_ref, kfs_ref, vv_ref, sg_ref, u_ref, sbc_ref, sbfin_ref, sffin_ref,
                   kb_scr, kfin_scr, v_scr, sb_scr, sf_scr, *, tm, k_scale, want_final_fwd):
    t = pl.program_id(1)
    nt = pl.num_programs(1)
    cpt = tm // CHUNK

    @pl.when(t == 0)
    def _():
        sb_scr[...] = sb0_ref[0]
        sf_scr[...] = jnp.zeros_like(sf_scr)

    shift = mod_ref[0, 0:1, :]
    scale = mod_ref[0, 1:2, :]
    h = (x_ref[0] * (1.0 + scale) + shift).astype(BF16)

    def proj(col):
        return jnp.dot(h, w_ref[:, col:col + RET_WIDTH], preferred_element_type=F32)

    cos = cos_ref[...]
    sin = sin_ref[...]
    even = (lax.broadcasted_iota(jnp.int32, (tm, LANES), 1) % 2) == 0

    def rope(p):
        swapped = jnp.where(even, pltpu.roll(p, LANES - 1, 1), pltpu.roll(p, 1, 1))
        return p * cos + swapped * sin

    def put_stacked(ref, j, t):
        left, right = _pair_masks(CHUNK)
        for c in range(cpt):
            tc = t[c * CHUNK:(c + 1) * CHUNK]
            ref[0, j, 2 * c * CHUNK:(2 * c + 1) * CHUNK, :] = jnp.where(left, tc, 0.0).astype(BF16)
            ref[0, j, (2 * c + 1) * CHUNK:(2 * c + 2) * CHUNK, :] = jnp.where(right, tc, 0.0).astype(BF16)

    pq = proj(0)
    pk = proj(RET_WIDTH)
    for j in range(N_PAIRS):
        lanes = slice(j * LANES, (j + 1) * LANES)
        qq = rope(pq[:, lanes])
        q_ref[0, j] = qq.astype(BF16)
        qf_ref[0, j] = (qq * rf_ref[:, lanes]).astype(BF16)
        qb_ref[0, j] = (qq * rb_ref[:, lanes]).astype(BF16)
        kk = rope(pk[:, lanes]) * k_scale
        put_stacked(kk_ref, j, kk)
        put_stacked(kfs_ref, j, kk * kdf_ref[:, lanes])
        kb_scr[:, lanes] = (kk * kdb_ref[:, lanes]).astype(BF16)
        if want_final_fwd:
            kfin_scr[:, lanes] = (kk * kfin_ref[:, lanes]).astype(BF16)

    pv = proj(2 * RET_WIDTH)
    v_scr[...] = pv.astype(BF16)
    g = proj(3 * RET_WIDTH)
    sg = (g * jax.nn.sigmoid(g)).astype(BF16)
    for j in range(N_PAIRS):
        lanes = slice(j * LANES, (j + 1) * LANES)
        put_stacked(vv_ref, j, pv[:, lanes])
        sg_ref[0, j] = sg[:, lanes]
    a = proj(4 * RET_WIDTH)
    gate = proj(5 * RET_WIDTH)
    u_ref[0] = (a * jax.nn.sigmoid(gate)).astype(BF16)

    bd = _block_diag_mask()
    contract_rows = (((0,), (0,)), ((), ()))
    for c in reversed(range(cpt)):
        rows = slice(c * CHUNK, (c + 1) * CHUNK)
        for j in range(N_PAIRS):
            lanes = slice(j * LANES, (j + 1) * LANES)
            vp = v_scr[rows, lanes]
            s = sb_scr[j]
            sbc_ref[0, c, j] = s.astype(BF16)
            kv = lax.dot_general(kb_scr[rows, lanes], vp, contract_rows, preferred_element_type=F32)
            sb_scr[j] = s * cdb_ref[:, lanes] + kv * bd
            if want_final_fwd:
                kvf = lax.dot_general(kfin_scr[rows, lanes], vp, contract_rows,
                                      preferred_element_type=F32)
                sf_scr[j] = sf_scr[j] + kvf * bd

    @pl.when(t == nt - 1)
    def _():
        sbfin_ref[0] = sb_scr[...]
        sffin_ref[0] = sf_scr[...]


def _inproj(layer, x, mod, cos_t, sin_t, w_in, tabs, kfin, sb0, *, tm, k_scale):
    b, n, _ = x.shape
    nt = n // tm
    cpt = tm // CHUNK
    rev = lambda bb, t: (bb, nt - 1 - t, 0)
    rev_tab = lambda bb, t: (nt - 1 - t, 0)
    seq = jax.ShapeDtypeStruct((b, N_PAIRS, n, LANES), BF16)
    state = jax.ShapeDtypeStruct((b, N_PAIRS, LANES, LANES), F32)
    seq_spec = pl.BlockSpec((1, N_PAIRS, tm, LANES), lambda bb, t: (bb, 0, nt - 1 - t, 0))
    stk = jax.ShapeDtypeStruct((b, N_PAIRS, 2 * n, LANES), BF16)
    stk_spec = pl.BlockSpec((1, N_PAIRS, 2 * tm, LANES), lambda bb, t: (bb, 0, nt - 1 - t, 0))
    state_spec = pl.BlockSpec((1, N_PAIRS, LANES, LANES), lambda bb, t: (bb, 0, 0, 0))
    want_final_fwd = kfin is not None
    if want_final_fwd:
        kfin_spec = pl.BlockSpec((tm, RET_WIDTH), rev_tab)
        kfin_rows = tm
    else:
        kfin = jnp.zeros((8, RET_WIDTH), F32)
        kfin_spec = _resident((8, RET_WIDTH))
        kfin_rows = 16
    return pl.pallas_call(
        functools.partial(_inproj_kernel, tm=tm, k_scale=k_scale, want_final_fwd=want_final_fwd),
        grid=(b, nt),
        in_specs=[pl.BlockSpec((1, tm, D_MODEL), rev),
                  pl.BlockSpec((1, 8, D_MODEL), lambda bb, t: (bb, 0, 0)),
                  pl.BlockSpec((tm, LANES), rev_tab),
                  pl.BlockSpec((tm, LANES), rev_tab),
                  _layer_resident(layer, (D_MODEL, IN_WIDTH)),
                  _resident((tm, RET_WIDTH)),
                  _resident((tm, RET_WIDTH)),
                  _resident((tm, RET_WIDTH)),
                  _resident((tm, RET_WIDTH)),
                  _resident((1, RET_WIDTH)),
                  kfin_spec,
                  state_spec],
        out_specs=[seq_spec] * 3 + [stk_spec] * 3 + [seq_spec, pl.BlockSpec((1, tm, CONV_CH), rev),
                   pl.BlockSpec((1, cpt, N_PAIRS, LANES, LANES),
                                lambda bb, t: (bb, nt - 1 - t, 0, 0, 0)),
                   state_spec, state_spec],
        out_shape=[seq] * 3 + [stk] * 3 + [seq, jax.ShapeDtypeStruct((b, n, CONV_CH), BF16),
                   jax.ShapeDtypeStruct((b, n // CHUNK, N_PAIRS, LANES, LANES), BF16),
                   state, state],
        scratch_shapes=[pltpu.VMEM((tm, RET_WIDTH), BF16),
                        pltpu.VMEM((kfin_rows, RET_WIDTH), BF16),
                        pltpu.VMEM((tm, RET_WIDTH), BF16),
                        pltpu.VMEM((N_PAIRS, LANES, LANES), F32),
                        pltpu.VMEM((N_PAIRS, LANES, LANES), F32)],
        compiler_params=_params(),
        name="inproj",
    )(x, mod, cos_t, sin_t, w_in, tabs["rf"], tabs["rb"], tabs["kdf"], tabs["kdb"], tabs["cdb"], kfin, sb0)


def _mixer_kernel(q_ref, qf_ref, qb_ref, kk_ref, kfs_ref, vv_ref, sg_ref, u_ref, up_ref, un_ref, sbc_ref,
                  sf0_ref, x_ref, mod_ref, mask_ref, cdf_ref, gn_ref, cw_ref, cv_ref, ones_ref, wout_ref, ln_ref,
                  o_ref, sffin_ref,
                  sf_scr, sfb_scr, pr_scr, o_scr, uf_scr, y_scr, mix_scr, *, tm):
    t = pl.program_id(1)
    nt = pl.num_programs(1)
    cpt = tm // CHUNK
    n_iter = cpt * N_PAIRS // 2

    @pl.when(t == 0)
    def _():
        sf_scr[...] = sf0_ref[0]
        sfb_scr[...] = sf0_ref[0].astype(BF16)

    for j in range(N_SLABS):
        lanes = slice(j * LANES, (j + 1) * LANES)
        uf_scr[j, 0:CONV_HALO, :] = jnp.where(t > 0, up_ref[0, :, lanes].astype(F32), 0.0)
        uf_scr[j, CONV_HALO:CONV_HALO + tm, :] = u_ref[0, :, lanes].astype(F32)
        uf_scr[j, CONV_HALO + tm:, :] = jnp.where(t < nt - 1, un_ref[0, :, lanes].astype(F32), 0.0)

    contract_rows = (((0,), (0,)), ((), ()))
    contract_lanes = (((1,), (1,)), ((), ()))

    def unit_of(it, e):
        c = it % cpt
        return (2 * (it // cpt) + e, pl.ds(pl.multiple_of(c * CHUNK, CHUNK), CHUNK),
                pl.ds(pl.multiple_of(2 * c * CHUNK, 2 * CHUNK), 2 * CHUNK))

    def probs_into(slot, it):
        for e in range(2):
            j, rows, srows = unit_of(it, e)
            sc = lax.dot_general(q_ref[0, j, rows, :], kk_ref[0, j, srows, :], contract_lanes,
                                 preferred_element_type=F32)
            pr_scr[slot, e] = (sc * mask_ref[j]).astype(BF16)

    def conv_unit(idx):
        slab = idx % N_SLABS
        base = (idx // N_SLABS) * CHUNK
        first = CONV_HALO - CONV_WIDTH // 2
        bias = jnp.broadcast_to(cw_ref[slab, CONV_WIDTH:CONV_WIDTH + 1, :], (HALF, LANES))
        accs = [[bias, None] for _ in range(2)]
        for tap in range(CONV_WIDTH):
            w = cw_ref[slab, tap:tap + 1, :]
            for par in range(2):
                term = w * uf_scr[slab, pl.ds(base + par + first + tap, HALF, stride=2), :]
                prev = accs[par][tap % 2]
                accs[par][tap % 2] = term if prev is None else prev + term
        for par in range(2):
            y_scr[slab, pl.ds(base + par, HALF, stride=2), :] = accs[par][0] + accs[par][1]

    probs_into(0, 0)

    def step(it, carry):
        c = it % cpt
        units = [unit_of(it, e) for e in range(2)]
        probs = [pr_scr[it % 2, e] for e in range(2)]
        states = [sf_scr[j] for j, _, _ in units]
        states_b = [sfb_scr[j] for j, _, _ in units]
        new_states = []
        for e, (j, rows, srows) in enumerate(units):
            vv = vv_ref[0, j, srows, :]
            lhs = jnp.concatenate([probs[e], qf_ref[0, j, rows, :], qb_ref[0, j, rows, :]], axis=1)
            rhs = jnp.concatenate([vv, states_b[e], sbc_ref[0, c, j]], axis=0)
            o_scr[j, rows, :] = jnp.dot(lhs, rhs, preferred_element_type=F32)
            kv = lax.dot_general(kfs_ref[0, j, srows, :], vv, contract_rows, preferred_element_type=F32)
            new_states.append(states[e] * cdf_ref[j] + kv)
        probs_into((it + 1) % 2, jnp.minimum(it + 1, n_iter - 1))
        for e, (j, _, _) in enumerate(units):
            sf_scr[j] = new_states[e]
            sfb_scr[j] = new_states[e].astype(BF16)
        conv_unit(2 * it)
        conv_unit(2 * it + 1)
        return carry

    lax.fori_loop(0, n_iter, step, 0)

    for i in range(2):
        lanes = slice(i * 2 * LANES, (i + 1) * 2 * LANES)
        o = jnp.concatenate([o_scr[2 * i], o_scr[2 * i + 1]], axis=1)
        d = o - jnp.dot(o.astype(BF16), ones_ref[...], preferred_element_type=F32)
        var = jnp.dot((d * d).astype(BF16), ones_ref[...], preferred_element_type=F32)
        on = d * lax.rsqrt(var + LN_EPS) * gn_ref[0:1, lanes] + gn_ref[1:2, lanes]
        sg = jnp.concatenate([sg_ref[0, 2 * i], sg_ref[0, 2 * i + 1]], axis=1).astype(F32)
        mix_scr[:, lanes] = (sg * on).astype(BF16)

    for c in range(cpt):
        rows = slice(c * CHUNK, (c + 1) * CHUNK)
        y = _layer_norm(jnp.concatenate([y_scr[j, rows, :] for j in range(N_SLABS)], axis=1),
                        cv_ref[0:1, :], cv_ref[1:2, :])
        mix_scr[rows, RET_WIDTH:] = (y * jax.nn.sigmoid(y)).astype(BF16)

    mix = jnp.dot(mix_scr[...], wout_ref[...], preferred_element_type=F32)
    gate = mod_ref[0, 2:3, :]
    o_ref[0] = _layer_norm(ALPHA * x_ref[0] + gate * mix, ln_ref[0:1, :], ln_ref[1:2, :])

    @pl.when(t == nt - 1)
    def _():
        sffin_ref[0] = sf_scr[...]


def _group_ones():
    g = jnp.arange(2 * LANES) // HEAD_DIM
    return jnp.where(g[:, None] == g[None, :], 1.0 / HEAD_DIM, 0.0).astype(BF16)


def _mixer(layer, q, qf, qb, kk, kfs, vv, sg, u, sbc, sf0, x, mod, tabs, gn, cw, cv, w_out, ln1, *, tm):
    b, n, _ = x.shape
    nt = n // tm
    cpt = tm // CHUNK
    hpt = tm // CONV_HALO
    last_halo = n // CONV_HALO - 1
    tile = lambda bb, t: (bb, t, 0)
    seq_spec = pl.BlockSpec((1, N_PAIRS, tm, LANES), lambda bb, t: (bb, 0, t, 0))
    stk_spec = pl.BlockSpec((1, N_PAIRS, 2 * tm, LANES), lambda bb, t: (bb, 0, t, 0))
    state_spec = pl.BlockSpec((1, N_PAIRS, LANES, LANES), lambda bb, t: (bb, 0, 0, 0))
    ones = _group_ones()
    return pl.pallas_call(
        functools.partial(_mixer_kernel, tm=tm),
        grid=(b, nt),
        in_specs=[seq_spec] * 3 + [stk_spec] * 3 + [seq_spec,
                  pl.BlockSpec((1, tm, CONV_CH), tile),
                  pl.BlockSpec((1, CONV_HALO, CONV_CH),
                               lambda bb, t: (bb, jnp.maximum(t * hpt - 1, 0), 0)),
                  pl.BlockSpec((1, CONV_HALO, CONV_CH),
                               lambda bb, t: (bb, jnp.minimum((t + 1) * hpt, last_halo), 0)),
                  pl.BlockSpec((1, cpt, N_PAIRS, LANES, LANES), lambda bb, t: (bb, t, 0, 0, 0)),
                  state_spec,
                  pl.BlockSpec((1, tm, D_MODEL), tile),
                  pl.BlockSpec((1, 8, D_MODEL), lambda bb, t: (bb, 0, 0)),
                  _resident((N_PAIRS, CHUNK, 2 * CHUNK)),
                  _resident((N_PAIRS, 1, LANES)),
                  _resident((2, RET_WIDTH)),
                  _resident((N_SLABS, 32, LANES)),
                  _resident((8, CONV_CH)),
                  _resident(ones.shape),
                  _layer_resident(layer, (D_MODEL, D_MODEL)),
                  _resident((2, D_MODEL))],
        out_specs=[pl.BlockSpec((1, tm, D_MODEL), tile), state_spec],
        out_shape=[jax.ShapeDtypeStruct((b, n, D_MODEL), F32),
                   jax.ShapeDtypeStruct((b, N_PAIRS, LANES, LANES), F32)],
        scratch_shapes=[pltpu.VMEM((N_PAIRS, LANES, LANES), F32),
                        pltpu.VMEM((N_PAIRS, LANES, LANES), BF16),
                        pltpu.VMEM((2, 2, CHUNK, 2 * CHUNK), BF16),
                        pltpu.VMEM((N_PAIRS, tm, LANES), F32),
                        pltpu.VMEM((N_SLABS, tm + 2 * CONV_HALO, LANES), F32),
                        pltpu.VMEM((N_SLABS, tm, LANES), F32),
                        pltpu.VMEM((tm, 2 * RET_WIDTH), BF16)],
        compiler_params=_params(),
        name="mixer",
    )(q, qf, qb, kk, kfs, vv, sg, u, u, u, sbc, sf0, x, mod, tabs["mask"],
      tabs["cdf"].reshape(N_PAIRS, 1, LANES), gn, cw, cv, ones, w_out, ln1)


def _mlp_kernel(x_ref, mod_ref, w1_ref, w2_ref, ln_ref, o_ref, *, sub):
    for i in range(x_ref.shape[1] // sub):
        rows = slice(i * sub, (i + 1) * sub)
        x = x_ref[0, rows, :]
        h = (x * (1.0 + mod_ref[0, 4:5, :]) + mod_ref[0, 3:4, :]).astype(BF16)
        y = jnp.zeros(x.shape, F32)
        for j in range(D_FF // FF_BLOCK):
            cols = slice(j * FF_BLOCK, (j + 1) * FF_BLOCK)
            a = jnp.maximum(jnp.dot(h, w1_ref[:, cols], preferred_element_type=F32), 0.0)
            y = y + jnp.dot((a * a).astype(BF16), w2_ref[cols, :], preferred_element_type=F32)
        o_ref[0, rows, :] = _layer_norm(ALPHA * x + mod_ref[0, 5:6, :] * y, ln_ref[0:1, :], ln_ref[1:2, :])


def _mlp(layer, x, mod, w1, w2, ln2, *, tm):
    b, n, _ = x.shape
    tile = lambda bb, t: (bb, t, 0)
    return pl.pallas_call(
        functools.partial(_mlp_kernel, sub=min(tm, MLP_ROWS)),
        grid=(b, n // tm),
        in_specs=[pl.BlockSpec((1, tm, D_MODEL), tile),
                  pl.BlockSpec((1, 8, D_MODEL), lambda bb, t: (bb, 0, 0)),
                  _layer_resident(layer, (D_MODEL, D_FF)),
                  _layer_resident(layer, (D_FF, D_MODEL)),
                  _resident((2, D_MODEL))],
        out_specs=pl.BlockSpec((1, tm, D_MODEL), tile),
        out_shape=jax.ShapeDtypeStruct((b, n, D_MODEL), F32),
        compiler_params=_params(),
        name="mlp",
    )(x, mod, w1, w2, ln2)


def _rope_tables(n):
    rows = n // GRID_W
    row = jnp.repeat(jnp.arange(rows, dtype=F32), GRID_W)
    col = jnp.tile(jnp.arange(GRID_W, dtype=F32), rows)
    n_freq = HEAD_DIM // 4
    inv = ROPE_BASE ** (-jnp.arange(n_freq, dtype=F32) / n_freq)
    ang = jnp.concatenate([row[:, None] * inv, col[:, None] * inv], axis=-1)
    cos = jnp.repeat(jnp.cos(ang), 2, axis=-1)
    sin = jnp.repeat(jnp.sin(ang), 2, axis=-1) * jnp.tile(jnp.array([-1.0, 1.0], F32), HEAD_DIM // 2)
    return jnp.tile(cos, (1, 2)), jnp.tile(sin, (1, 2))


def _per_lane(t):
    return jnp.repeat(t.T, HEAD_DIM, axis=1)


def _decay_tables(lg_f, lg_b, tm):
    idx = jnp.arange(CHUNK, dtype=F32)
    lf = lg_f.astype(F32)[:, None]
    lb = lg_b.astype(F32)[:, None]
    diff = idx[:, None] - idx[None, :]
    fwd = jnp.where(diff >= 0, jnp.exp(lf[:, :, None] * jnp.maximum(diff, 0.0)), 0.0)
    bwd = jnp.where(diff <= 0, jnp.exp(lb[:, :, None] * jnp.maximum(-diff, 0.0)), 0.0)
    mask = (fwd + bwd).reshape(N_PAIRS, 2, CHUNK, CHUNK).transpose(0, 2, 1, 3)
    reps = tm // CHUNK
    return {
        "mask": mask.reshape(N_PAIRS, CHUNK, 2 * CHUNK),
        "rf": jnp.tile(_per_lane(jnp.exp(lf * (idx + 1.0))), (reps, 1)),
        "rb": jnp.tile(_per_lane(jnp.exp(lb * (CHUNK - idx))), (reps, 1)),
        "kdf": jnp.tile(_per_lane(jnp.exp(lf * (CHUNK - 1.0 - idx))), (reps, 1)),
        "kdb": jnp.tile(_per_lane(jnp.exp(lb * idx)), (reps, 1)),
        "cdf": _per_lane(jnp.exp(lf * CHUNK)),
        "cdb": _per_lane(jnp.exp(lb * CHUNK)),
    }


def _final_weights(lg_f, n):
    pos = jnp.arange(n, dtype=F32)
    return _per_lane(jnp.exp(lg_f.astype(F32)[:, None] * (n - 1.0 - pos)))


def _rows8(*rows):
    out = jnp.stack(rows, axis=-2)
    pad = [(0, 0)] * (out.ndim - 2) + [(0, 8 - len(rows)), (0, 0)]
    return jnp.pad(out, pad)


def kernel(x, c, ctx, c_ctx, w_ada, b_ada, w_in, ret_log_rate_fwd, ret_log_rate_bwd, ret_gn_w, ret_gn_b,
           conv_w, conv_b, conv_ln_w, conv_ln_b, w_out, ln1_w, ln1_b, w_ff1, w_ff2, ln2_w, ln2_b):
    b, n, _ = x.shape
    n_ctx = ctx.shape[1]
    tm_lat, tm_ctx, tm_mlp = 512, n_ctx, 1024
    w_in, w_out, w_ff1, w_ff2 = (w.astype(BF16) for w in (w_in, w_out, w_ff1, w_ff2))

    cond = jnp.zeros((8, D_MODEL), F32).at[:b].set(c).at[b].set(c_ctx)
    ada = _adaln(cond, w_ada, b_ada).reshape(DEPTH, 8, 6, D_MODEL)

    cos_lat, sin_lat = _rope_tables(n)
    cos_ctx = jnp.ones((n_ctx, LANES), F32)
    sin_ctx = jnp.zeros((n_ctx, LANES), F32)
    zero_state = jnp.zeros((b, N_PAIRS, LANES, LANES), F32)

    for l in range(DEPTH):
        last = l == DEPTH - 1
        mod_lat = jnp.pad(ada[l, :b], ((0, 0), (0, 2), (0, 0)))
        mod_ctx = jnp.broadcast_to(jnp.pad(ada[l, b], ((0, 2), (0, 0))), (b, 8, D_MODEL))
        lg_f = -jnp.exp(ret_log_rate_fwd[l])
        lg_b = -jnp.exp(ret_log_rate_bwd[l])
        tabs_lat = _decay_tables(lg_f, lg_b, tm_lat)
        tabs_ctx = _decay_tables(lg_f, lg_b, tm_ctx)
        gn = jnp.stack([ret_gn_w[l], ret_gn_b[l]])
        cw = jnp.concatenate([conv_w[l], conv_b[l][None]], axis=0).reshape(32, N_SLABS, LANES).transpose(1, 0, 2)
        cv = _rows8(conv_ln_w[l], conv_ln_b[l])
        ln1 = jnp.stack([ln1_w[l], ln1_b[l]])
        ln2 = jnp.stack([ln2_w[l], ln2_b[l]])

        k_scale_ctx = 1.0 if last else HEAD_DIM ** -0.5
        (qc, qfc, qbc, kkc, kfsc, vvc, sgc, uc, sbc_c, sb_fin, sf_fin) = _inproj(
            l, ctx, mod_ctx, cos_ctx, sin_ctx, w_in, tabs_ctx, _final_weights(lg_f, n_ctx), zero_state,
            tm=tm_ctx, k_scale=k_scale_ctx)

        (q, qf, qb, kk, kfs, vv, sg, u, sbc, _, _) = _inproj(
            l, x, mod_lat, cos_lat, sin_lat, w_in, tabs_lat, None, sb_fin,
            tm=tm_lat, k_scale=HEAD_DIM ** -0.5)
        x, _ = _mixer(l, q, qf, qb, kk, kfs, vv, sg, u, sbc, sf_fin, x, mod_lat, tabs_lat, gn, cw, cv, w_out, ln1,
                      tm=tm_lat)
        x = _mlp(l, x, mod_lat, w_ff1, w_ff2, ln2, tm=tm_mlp)

        if not last:
            ctx, _ = _mixer(l, qc, qfc, qbc, kkc, kfsc, vvc, sgc, uc, sbc_c, zero_state, ctx, mod_ctx, tabs_ctx, gn,
                            cw, cv, w_out, ln1, tm=tm_ctx)
            ctx = _mlp(l, ctx, mod_ctx, w_ff1, w_ff2, ln2, tm=tm_ctx)

    return x
```

```python
import functools

import jax
import jax.numpy as jnp
from jax import lax
from jax.experimental import pallas as pl
from jax.experimental.pallas import tpu as pltpu

D_MODEL = 1024
DEPTH = 4
GRID_W = 64
RET_HEADS = 8
HEAD_DIM = 64
RET_WIDTH = RET_HEADS * HEAD_DIM
CONV_CH = 512
CONV_WIDTH = 31
CONV_HALO = 16
IN_WIDTH = 3072
D_FF = 4 * D_MODEL
CHUNK = 128
ROPE_BASE = 10000.0
LN_EPS = 1e-5
ALPHA = (2 * DEPTH) ** 0.25
LANES = 128
N_PAIRS = RET_HEADS // 2
FF_BLOCK = 1024
ROW_GROUP = 256
N_SLABS = CONV_CH // LANES
HALF = CHUNK // 2
VMEM_LIMIT = 56 * 1024 * 1024

F32 = jnp.float32
BF16 = jnp.bfloat16


def _resident(shape):
    zeros = (0,) * len(shape)
    return pl.BlockSpec(shape, lambda *_: zeros, pipeline_mode=pl.Buffered(1))


def _layer_resident(layer, shape):
    zeros = (0,) * len(shape)
    return pl.BlockSpec((None,) + tuple(shape), lambda *_: (layer,) + zeros, pipeline_mode=pl.Buffered(1))


def _params():
    return pltpu.CompilerParams(dimension_semantics=("arbitrary", "arbitrary"),
                                vmem_limit_bytes=VMEM_LIMIT)


def _layer_norm(x, w, b):
    mu = jnp.mean(x, axis=-1, keepdims=True)
    d = x - mu
    var = jnp.mean(d * d, axis=-1, keepdims=True)
    return d * lax.rsqrt(var + LN_EPS) * w + b


def _pair_masks(rows):
    lane = lax.broadcasted_iota(jnp.int32, (rows, LANES), 1)
    return lane < HEAD_DIM, lane >= HEAD_DIM


def _block_diag_mask():
    r = lax.broadcasted_iota(jnp.int32, (LANES, LANES), 0)
    c = lax.broadcasted_iota(jnp.int32, (LANES, LANES), 1)
    return ((r < HEAD_DIM) == (c < HEAD_DIM)).astype(F32)


def _adaln_kernel(cond_ref, w_ref, b_ref, o_ref):
    cond = cond_ref[...]
    act = cond * jax.nn.sigmoid(cond)
    o_ref[0] = jnp.dot(act, w_ref[0], preferred_element_type=F32,
                       precision=lax.Precision.HIGHEST) + b_ref[0]


def _adaln(cond, w_ada, b_ada):
    width = w_ada.shape[-1]
    block = 1536
    return pl.pallas_call(
        _adaln_kernel,
        grid=(DEPTH, width // block),
        in_specs=[pl.BlockSpec((8, D_MODEL), lambda l, j: (0, 0)),
                  pl.BlockSpec((1, D_MODEL, block), lambda l, j: (l, 0, j)),
                  pl.BlockSpec((1, 1, block), lambda l, j: (l, 0, j))],
        out_specs=pl.BlockSpec((1, 8, block), lambda l, j: (l, 0, j)),
        out_shape=jax.ShapeDtypeStruct((DEPTH, 8, width), F32),
        compiler_params=_params(),
        name="adaln",
    )(cond, w_ada, b_ada.reshape(DEPTH, 1, width))


def _inproj_kernel(x_ref, mod_ref, cos_ref, sin_ref, w_ref, rf_ref, rb_ref, kdf_ref, kdb_ref, cdb_ref,
                   kfin_ref, sb0_ref,
                   q_ref, qf_ref, qb_ref, kk_ref, kfs_ref, vv_ref, sg_ref, u_ref, sbc_ref, sbfin_ref, sffin_ref,
                   kb_scr, kfin_scr, v_scr, sb_scr, sf_scr, *, tm, k_scale, want_final_fwd):
    t = pl.program_id(1)
    nt = pl.num_programs(1)
    cpt = tm // CHUNK

    @pl.when(t == 0)
    def _():
        sb_scr[...] = sb0_ref[0]
        sf_scr[...] = jnp.zeros_like(sf_scr)

    shift = mod_ref[0, 0:1, :]
    scale = mod_ref[0, 1:2, :]
    left, right = _pair_masks(CHUNK)
    even = (lax.broadcasted_iota(jnp.int32, (ROW_GROUP, LANES), 1) % 2) == 0
    for r0 in range(0, tm, ROW_GROUP):
        rws = slice(r0, r0 + ROW_GROUP)
        h = (x_ref[0, rws, :] * (1.0 + scale) + shift).astype(BF16)
        cos = cos_ref[rws, :]
        sin = sin_ref[rws, :]

        def proj(col):
            return jnp.dot(h, w_ref[:, col:col + RET_WIDTH], preferred_element_type=F32)

        def rope(p):
            swapped = jnp.where(even, pltpu.roll(p, LANES - 1, 1), pltpu.roll(p, 1, 1))
            return p * cos + swapped * sin

        def put_stacked(ref, j, t):
            for c in range(ROW_GROUP // CHUNK):
                tc = t[c * CHUNK:(c + 1) * CHUNK]
                o = 2 * (r0 + c * CHUNK)
                ref[0, j, o:o + CHUNK, :] = jnp.where(left, tc, 0.0).astype(BF16)
                ref[0, j, o + CHUNK:o + 2 * CHUNK, :] = jnp.where(right, tc, 0.0).astype(BF16)

        pq = proj(0)
        pk = proj(RET_WIDTH)
        for j in range(N_PAIRS):
            lanes = slice(j * LANES, (j + 1) * LANES)
            qq = rope(pq[:, lanes])
            q_ref[0, j, rws, :] = qq.astype(BF16)
            qf_ref[0, j, rws, :] = (qq * rf_ref[rws, lanes]).astype(BF16)
            qb_ref[0, j, rws, :] = (qq * rb_ref[rws, lanes]).astype(BF16)
            kk = rope(pk[:, lanes]) * k_scale
            put_stacked(kk_ref, j, kk)
            put_stacked(kfs_ref, j, kk * kdf_ref[rws, lanes])
            kb_scr[rws, lanes] = (kk * kdb_ref[rws, lanes]).astype(BF16)
            if want_final_fwd:
                kfin_scr[rws, lanes] = (kk * kfin_ref[rws, lanes]).astype(BF16)

        pv = proj(2 * RET_WIDTH)
        v_scr[rws, :] = pv.astype(BF16)
        g = proj(3 * RET_WIDTH)
        sg = (g * jax.nn.sigmoid(g)).astype(BF16)
        for j in range(N_PAIRS):
            lanes = slice(j * LANES, (j + 1) * LANES)
            put_stacked(vv_ref, j, pv[:, lanes])
            sg_ref[0, j, rws, :] = sg[:, lanes]
        a = proj(4 * RET_WIDTH)
        gate = proj(5 * RET_WIDTH)
        u_ref[0, rws, :] = (a * jax.nn.sigmoid(gate)).astype(BF16)

    bd = _block_diag_mask()
    contract_rows = (((0,), (0,)), ((), ()))
    for c in reversed(range(cpt)):
        rows = slice(c * CHUNK, (c + 1) * CHUNK)
        for j in range(N_PAIRS):
            lanes = slice(j * LANES, (j + 1) * LANES)
            vp = v_scr[rows, lanes]
            s = sb_scr[j]
            sbc_ref[0, c, j] = s.astype(BF16)
            kv = lax.dot_general(kb_scr[rows, lanes], vp, contract_rows, preferred_element_type=F32)
            sb_scr[j] = s * cdb_ref[:, lanes] + kv * bd
            if want_final_fwd:
                kvf = lax.dot_general(kfin_scr[rows, lanes], vp, contract_rows,
                                      preferred_element_type=F32)
                sf_scr[j] = sf_scr[j] + kvf * bd

    @pl.when(t == nt - 1)
    def _():
        sbfin_ref[0] = sb_scr[...]
        sffin_ref[0] = sf_scr[...]


def _inproj(layer, x, mod, cos_t, sin_t, w_in, tabs, kfin, sb0, *, tm, k_scale):
    b, n, _ = x.shape
    nt = n // tm
    cpt = tm // CHUNK
    rev = lambda bb, t: (bb, nt - 1 - t, 0)
    rev_tab = lambda bb, t: (nt - 1 - t, 0)
    seq = jax.ShapeDtypeStruct((b, N_PAIRS, n, LANES), BF16)
    state = jax.ShapeDtypeStruct((b, N_PAIRS, LANES, LANES), F32)
    seq_spec = pl.BlockSpec((1, N_PAIRS, tm, LANES), lambda bb, t: (bb, 0, nt - 1 - t, 0))
    stk = jax.ShapeDtypeStruct((b, N_PAIRS, 2 * n, LANES), BF16)
    stk_spec = pl.BlockSpec((1, N_PAIRS, 2 * tm, LANES), lambda bb, t: (bb, 0, nt - 1 - t, 0))
    state_spec = pl.BlockSpec((1, N_PAIRS, LANES, LANES), lambda bb, t: (bb, 0, 0, 0))
    want_final_fwd = kfin is not None
    if want_final_fwd:
        kfin_spec = pl.BlockSpec((tm, RET_WIDTH), rev_tab)
        kfin_rows = tm
    else:
        kfin = jnp.zeros((8, RET_WIDTH), F32)
        kfin_spec = _resident((8, RET_WIDTH))
        kfin_rows = 16
    return pl.pallas_call(
        functools.partial(_inproj_kernel, tm=tm, k_scale=k_scale, want_final_fwd=want_final_fwd),
        grid=(b, nt),
        in_specs=[pl.BlockSpec((1, tm, D_MODEL), rev),
                  pl.BlockSpec((1, 8, D_MODEL), lambda bb, t: (bb, 0, 0)),
                  pl.BlockSpec((tm, LANES), rev_tab),
                  pl.BlockSpec((tm, LANES), rev_tab),
                  _layer_resident(layer, (D_MODEL, IN_WIDTH)),
                  _resident((tm, RET_WIDTH)),
                  _resident((tm, RET_WIDTH)),
                  _resident((tm, RET_WIDTH)),
                  _resident((tm, RET_WIDTH)),
                  _resident((1, RET_WIDTH)),
                  kfin_spec,
                  state_spec],
        out_specs=[seq_spec] * 3 + [stk_spec] * 3 + [seq_spec, pl.BlockSpec((1, tm, CONV_CH), rev),
                   pl.BlockSpec((1, cpt, N_PAIRS, LANES, LANES),
                                lambda bb, t: (bb, nt - 1 - t, 0, 0, 0)),
                   state_spec, state_spec],
        out_shape=[seq] * 3 + [stk] * 3 + [seq, jax.ShapeDtypeStruct((b, n, CONV_CH), BF16),
                   jax.ShapeDtypeStruct((b, n // CHUNK, N_PAIRS, LANES, LANES), BF16),
                   state, state],
        scratch_shapes=[pltpu.VMEM((tm, RET_WIDTH), BF16),
                        pltpu.VMEM((kfin_rows, RET_WIDTH), BF16),
                        pltpu.VMEM((tm, RET_WIDTH), BF16),
                        pltpu.VMEM((N_PAIRS, LANES, LANES), F32),
                        pltpu.VMEM((N_PAIRS, LANES, LANES), F32)],
        compiler_params=_params(),
        name="inproj",
    )(x, mod, cos_t, sin_t, w_in, tabs["rf"], tabs["rb"], tabs["kdf"], tabs["kdb"], tabs["cdb"], kfin, sb0)


def _mixer_kernel(q_ref, qf_ref, qb_ref, kk_ref, kfs_ref, vv_ref, sg_ref, u_ref, up_ref, un_ref, sbc_ref,
                  sf0_ref, x_ref, mod_ref, mask_ref, cdf_ref, gn_ref, cw_ref, cv_ref, ones_ref, wout_ref,
                  o_ref, sffin_ref,
                  sf_scr, sfb_scr, pr_scr, o_scr, uf_scr, y_scr, mix_scr, *, tm):
    t = pl.program_id(1)
    nt = pl.num_programs(1)
    cpt = tm // CHUNK
    n_iter = cpt * N_PAIRS // 2

    @pl.when(t == 0)
    def _():
        sf_scr[...] = sf0_ref[0]
        sfb_scr[...] = sf0_ref[0].astype(BF16)

    contract_rows = (((0,), (0,)), ((), ()))
    contract_lanes = (((1,), (1,)), ((), ()))

    def unit_of(it, e):
        c = it % cpt
        return (2 * (it // cpt) + e, pl.ds(pl.multiple_of(c * CHUNK, CHUNK), CHUNK),
                pl.ds(pl.multiple_of(2 * c * CHUNK, 2 * CHUNK), 2 * CHUNK))

    def probs_into(slot, it):
        for e in range(2):
            j, rows, srows = unit_of(it, e)
            sc = lax.dot_general(q_ref[0, j, rows, :], kk_ref[0, j, srows, :], contract_lanes,
                                 preferred_element_type=F32)
            pr_scr[slot, e] = (sc * mask_ref[j]).astype(BF16)

    def conv_unit(idx):
        slab = idx % N_SLABS
        base = (idx // N_SLABS) * CHUNK
        first = CONV_HALO - CONV_WIDTH // 2
        bias = jnp.broadcast_to(cw_ref[slab, CONV_WIDTH:CONV_WIDTH + 1, :], (HALF, LANES))
        accs = [[bias, None] for _ in range(2)]
        for tap in range(CONV_WIDTH):
            w = cw_ref[slab, tap:tap + 1, :]
            for par in range(2):
                term = w * uf_scr[slab, pl.ds(base + par + first + tap, HALF, stride=2), :]
                prev = accs[par][tap % 2]
                accs[par][tap % 2] = term if prev is None else prev + term
        for par in range(2):
            y_scr[slab, pl.ds(base + par, HALF, stride=2), :] = accs[par][0] + accs[par][1]

    probs_into(0, 0)

    for j in range(N_SLABS):
        lanes = slice(j * LANES, (j + 1) * LANES)
        uf_scr[j, 0:CONV_HALO, :] = jnp.where(t > 0, up_ref[0, :, lanes].astype(F32), 0.0)
        uf_scr[j, CONV_HALO:CONV_HALO + tm, :] = u_ref[0, :, lanes].astype(F32)
        uf_scr[j, CONV_HALO + tm:, :] = jnp.where(t < nt - 1, un_ref[0, :, lanes].astype(F32), 0.0)

    def step(it, carry):
        c = it % cpt
        units = [unit_of(it, e) for e in range(2)]
        probs = [pr_scr[it % 2, e] for e in range(2)]
        states = [sf_scr[j] for j, _, _ in units]
        states_b = [sfb_scr[j] for j, _, _ in units]
        new_states = []
        for e, (j, rows, srows) in enumerate(units):
            vv = vv_ref[0, j, srows, :]
            lhs = jnp.concatenate([probs[e], qf_ref[0, j, rows, :], qb_ref[0, j, rows, :]], axis=1)
            rhs = jnp.concatenate([vv, states_b[e], sbc_ref[0, c, j]], axis=0)
            o_scr[j, rows, :] = jnp.dot(lhs, rhs, preferred_element_type=F32)
            kv = lax.dot_general(kfs_ref[0, j, srows, :], vv, contract_rows, preferred_element_type=F32)
            new_states.append(states[e] * cdf_ref[j] + kv)
        probs_into((it + 1) % 2, jnp.minimum(it + 1, n_iter - 1))
        for e, (j, _, _) in enumerate(units):
            sf_scr[j] = new_states[e]
            sfb_scr[j] = new_states[e].astype(BF16)
        conv_unit(2 * it)
        conv_unit(2 * it + 1)
        return carry

    lax.fori_loop(0, n_iter, step, 0)

    gate = mod_ref[0, 2:3, :]
    for g in range(tm // ROW_GROUP):
        rows = slice(g * ROW_GROUP, (g + 1) * ROW_GROUP)
        for i in range(2):
            lanes = slice(i * 2 * LANES, (i + 1) * 2 * LANES)
            o = jnp.concatenate([o_scr[2 * i, rows, :], o_scr[2 * i + 1, rows, :]], axis=1)
            d = o - jnp.dot(o.astype(BF16), ones_ref[...], preferred_element_type=F32)
            var = jnp.dot((d * d).astype(BF16), ones_ref[...], preferred_element_type=F32)
            on = d * lax.rsqrt(var + LN_EPS) * gn_ref[0:1, lanes] + gn_ref[1:2, lanes]
            sg = jnp.concatenate([sg_ref[0, 2 * i, rows, :], sg_ref[0, 2 * i + 1, rows, :]], axis=1).astype(F32)
            mix_scr[rows, lanes] = (sg * on).astype(BF16)
        y = _layer_norm(jnp.concatenate([y_scr[j, rows, :] for j in range(N_SLABS)], axis=1),
                        cv_ref[0:1, :], cv_ref[1:2, :])
        mix_scr[rows, RET_WIDTH:] = (y * jax.nn.sigmoid(y)).astype(BF16)
        mix = jnp.dot(mix_scr[rows, :], wout_ref[...], preferred_element_type=F32)
        o_ref[0, rows, :] = ALPHA * x_ref[0, rows, :] + gate * mix

    @pl.when(t == nt - 1)
    def _():
        sffin_ref[0] = sf_scr[...]


def _group_ones():
    g = jnp.arange(2 * LANES) // HEAD_DIM
    return jnp.where(g[:, None] == g[None, :], 1.0 / HEAD_DIM, 0.0).astype(BF16)


def _mixer(layer, q, qf, qb, kk, kfs, vv, sg, u, sbc, sf0, x, mod, tabs, gn, cw, cv, w_out, *, tm):
    b, n, _ = x.shape
    nt = n // tm
    cpt = tm // CHUNK
    hpt = tm // CONV_HALO
    last_halo = n // CONV_HALO - 1
    tile = lambda bb, t: (bb, t, 0)
    seq_spec = pl.BlockSpec((1, N_PAIRS, tm, LANES), lambda bb, t: (bb, 0, t, 0))
    stk_spec = pl.BlockSpec((1, N_PAIRS, 2 * tm, LANES), lambda bb, t: (bb, 0, t, 0))
    state_spec = pl.BlockSpec((1, N_PAIRS, LANES, LANES), lambda bb, t: (bb, 0, 0, 0))
    ones = _group_ones()
    return pl.pallas_call(
        functools.partial(_mixer_kernel, tm=tm),
        grid=(b, nt),
        in_specs=[seq_spec] * 3 + [stk_spec] * 3 + [seq_spec,
                  pl.BlockSpec((1, tm, CONV_CH), tile),
                  pl.BlockSpec((1, CONV_HALO, CONV_CH),
                               lambda bb, t: (bb, jnp.maximum(t * hpt - 1, 0), 0)),
                  pl.BlockSpec((1, CONV_HALO, CONV_CH),
                               lambda bb, t: (bb, jnp.minimum((t + 1) * hpt, last_halo), 0)),
                  pl.BlockSpec((1, cpt, N_PAIRS, LANES, LANES), lambda bb, t: (bb, t, 0, 0, 0)),
                  state_spec,
                  pl.BlockSpec((1, tm, D_MODEL), tile),
                  pl.BlockSpec((1, 8, D_MODEL), lambda bb, t: (bb, 0, 0)),
                  _resident((N_PAIRS, CHUNK, 2 * CHUNK)),
                  _resident((N_PAIRS, 1, LANES)),
                  _resident((2, RET_WIDTH)),
                  _resident((N_SLABS, 32, LANES)),
                  _resident((8, CONV_CH)),
                  _resident(ones.shape),
                  _layer_resident(layer, (D_MODEL, D_MODEL))],
        out_specs=[pl.BlockSpec((1, tm, D_MODEL), tile), state_spec],
        out_shape=[jax.ShapeDtypeStruct((b, n, D_MODEL), F32),
                   jax.ShapeDtypeStruct((b, N_PAIRS, LANES, LANES), F32)],
        scratch_shapes=[pltpu.VMEM((N_PAIRS, LANES, LANES), F32),
                        pltpu.VMEM((N_PAIRS, LANES, LANES), BF16),
                        pltpu.VMEM((2, 2, CHUNK, 2 * CHUNK), BF16),
                        pltpu.VMEM((N_PAIRS, tm, LANES), F32),
                        pltpu.VMEM((N_SLABS, tm + 2 * CONV_HALO, LANES), F32),
                        pltpu.VMEM((N_SLABS, tm, LANES), F32),
                        pltpu.VMEM((tm, 2 * RET_WIDTH), BF16)],
        compiler_params=_params(),
        name="mixer",
    )(q, qf, qb, kk, kfs, vv, sg, u, u, u, sbc, sf0, x, mod, tabs["mask"],
      tabs["cdf"].reshape(N_PAIRS, 1, LANES), gn, cw, cv, ones, w_out)


def _mlp_kernel(z_ref, mod_ref, w1_ref, w2_ref, ln1_ref, ln_ref, o_ref, *, sub):
    for i in range(z_ref.shape[1] // sub):
        rows = slice(i * sub, (i + 1) * sub)
        x = _layer_norm(z_ref[0, rows, :], ln1_ref[0:1, :], ln1_ref[1:2, :])
        h = (x * (1.0 + mod_ref[0, 4:5, :]) + mod_ref[0, 3:4, :]).astype(BF16)
        y = jnp.zeros(x.shape, F32)
        for j in range(D_FF // FF_BLOCK):
            cols = slice(j * FF_BLOCK, (j + 1) * FF_BLOCK)
            a = jnp.maximum(jnp.dot(h, w1_ref[:, cols], preferred_element_type=F32), 0.0)
            y = y + jnp.dot((a * a).astype(BF16), w2_ref[cols, :], preferred_element_type=F32)
        o_ref[0, rows, :] = _layer_norm(ALPHA * x + mod_ref[0, 5:6, :] * y, ln_ref[0:1, :], ln_ref[1:2, :])


def _mlp(layer, z, mod, w1, w2, ln1, ln2, *, tm):
    b, n, _ = z.shape
    tile = lambda bb, t: (bb, t, 0)
    return pl.pallas_call(
        functools.partial(_mlp_kernel, sub=min(tm, ROW_GROUP)),
        grid=(b, n // tm),
        in_specs=[pl.BlockSpec((1, tm, D_MODEL), tile),
                  pl.BlockSpec((1, 8, D_MODEL), lambda bb, t: (bb, 0, 0)),
                  _layer_resident(layer, (D_MODEL, D_FF)),
                  _layer_resident(layer, (D_FF, D_MODEL)),
                  _resident((2, D_MODEL)),
                  _resident((2, D_MODEL))],
        out_specs=pl.BlockSpec((1, tm, D_MODEL), tile),
        out_shape=jax.ShapeDtypeStruct((b, n, D_MODEL), F32),
        compiler_params=_params(),
        name="mlp",
    )(z, mod, w1, w2, ln1, ln2)


def _rope_tables(n):
    rows = n // GRID_W
    row = jnp.repeat(jnp.arange(rows, dtype=F32), GRID_W)
    col = jnp.tile(jnp.arange(GRID_W, dtype=F32), rows)
    n_freq = HEAD_DIM // 4
    inv = ROPE_BASE ** (-jnp.arange(n_freq, dtype=F32) / n_freq)
    ang = jnp.concatenate([row[:, None] * inv, col[:, None] * inv], axis=-1)
    cos = jnp.repeat(jnp.cos(ang), 2, axis=-1)
    sin = jnp.repeat(jnp.sin(ang), 2, axis=-1) * jnp.tile(jnp.array([-1.0, 1.0], F32), HEAD_DIM // 2)
    return jnp.tile(cos, (1, 2)), jnp.tile(sin, (1, 2))


def _per_lane(t):
    return jnp.repeat(t.T, HEAD_DIM, axis=1)


def _decay_tables(lg_f, lg_b, tm):
    idx = jnp.arange(CHUNK, dtype=F32)
    lf = lg_f.astype(F32)[:, None]
    lb = lg_b.astype(F32)[:, None]
    diff = idx[:, None] - idx[None, :]
    fwd = jnp.where(diff >= 0, jnp.exp(lf[:, :, None] * jnp.maximum(diff, 0.0)), 0.0)
    bwd = jnp.where(diff <= 0, jnp.exp(lb[:, :, None] * jnp.maximum(-diff, 0.0)), 0.0)
    mask = (fwd + bwd).reshape(N_PAIRS, 2, CHUNK, CHUNK).transpose(0, 2, 1, 3)
    reps = tm // CHUNK
    return {
        "mask": mask.reshape(N_PAIRS, CHUNK, 2 * CHUNK),
        "rf": jnp.tile(_per_lane(jnp.exp(lf * (idx + 1.0))), (reps, 1)),
        "rb": jnp.tile(_per_lane(jnp.exp(lb * (CHUNK - idx))), (reps, 1)),
        "kdf": jnp.tile(_per_lane(jnp.exp(lf * (CHUNK - 1.0 - idx))), (reps, 1)),
        "kdb": jnp.tile(_per_lane(jnp.exp(lb * idx)), (reps, 1)),
        "cdf": _per_lane(jnp.exp(lf * CHUNK)),
        "cdb": _per_lane(jnp.exp(lb * CHUNK)),
    }


def _final_weights(lg_f, n):
    pos = jnp.arange(n, dtype=F32)
    return _per_lane(jnp.exp(lg_f.astype(F32)[:, None] * (n - 1.0 - pos)))


def _rows8(*rows):
    out = jnp.stack(rows, axis=-2)
    pad = [(0, 0)] * (out.ndim - 2) + [(0, 8 - len(rows)), (0, 0)]
    return jnp.pad(out, pad)


def kernel(x, c, ctx, c_ctx, w_ada, b_ada, w_in, ret_log_rate_fwd, ret_log_rate_bwd, ret_gn_w, ret_gn_b,
           conv_w, conv_b, conv_ln_w, conv_ln_b, w_out, ln1_w, ln1_b, w_ff1, w_ff2, ln2_w, ln2_b):
    b, n, _ = x.shape
    n_ctx = ctx.shape[1]
    tm_lat, tm_ctx, tm_mlp = 512, n_ctx, 1024
    w_in, w_out, w_ff1, w_ff2 = (w.astype(BF16) for w in (w_in, w_out, w_ff1, w_ff2))

    cond = jnp.zeros((8, D_MODEL), F32).at[:b].set(c).at[b].set(c_ctx)
    ada = _adaln(cond, w_ada, b_ada).reshape(DEPTH, 8, 6, D_MODEL)

    cos_lat, sin_lat = _rope_tables(n)
    cos_ctx = jnp.ones((n_ctx, LANES), F32)
    sin_ctx = jnp.zeros((n_ctx, LANES), F32)
    zero_state = jnp.zeros((b, N_PAIRS, LANES, LANES), F32)

    for l in range(DEPTH):
        last = l == DEPTH - 1
        mod_lat = jnp.pad(ada[l, :b], ((0, 0), (0, 2), (0, 0)))
        mod_ctx = jnp.broadcast_to(jnp.pad(ada[l, b], ((0, 2), (0, 0))), (b, 8, D_MODEL))
        lg_f = -jnp.exp(ret_log_rate_fwd[l])
        lg_b = -jnp.exp(ret_log_rate_bwd[l])
        tabs_lat = _decay_tables(lg_f, lg_b, tm_lat)
        tabs_ctx = _decay_tables(lg_f, lg_b, tm_ctx)
        gn = jnp.stack([ret_gn_w[l], ret_gn_b[l]])
        cw = jnp.concatenate([conv_w[l], conv_b[l][None]], axis=0).reshape(32, N_SLABS, LANES).transpose(1, 0, 2)
        cv = _rows8(conv_ln_w[l], conv_ln_b[l])
        ln1 = jnp.stack([ln1_w[l], ln1_b[l]])
        ln2 = jnp.stack([ln2_w[l], ln2_b[l]])

        k_scale_ctx = 1.0 if last else HEAD_DIM ** -0.5
        (qc, qfc, qbc, kkc, kfsc, vvc, sgc, uc, sbc_c, sb_fin, sf_fin) = _inproj(
            l, ctx, mod_ctx, cos_ctx, sin_ctx, w_in, tabs_ctx, _final_weights(lg_f, n_ctx), zero_state,
            tm=tm_ctx, k_scale=k_scale_ctx)

        (q, qf, qb, kk, kfs, vv, sg, u, sbc, _, _) = _inproj(
            l, x, mod_lat, cos_lat, sin_lat, w_in, tabs_lat, None, sb_fin,
            tm=tm_lat, k_scale=HEAD_DIM ** -0.5)
        z, _ = _mixer(l, q, qf, qb, kk, kfs, vv, sg, u, sbc, sf_fin, x, mod_lat, tabs_lat, gn, cw, cv, w_out,
                      tm=tm_lat)
        x = _mlp(l, z, mod_lat, w_ff1, w_ff2, ln1, ln2, tm=tm_mlp)

        if not last:
            zc, _ = _mixer(l, qc, qfc, qbc, kkc, kfsc, vvc, sgc, uc, sbc_c, zero_state, ctx, mod_ctx, tabs_ctx, gn,
                           cw, cv, w_out, tm=tm_ctx)
            ctx = _mlp(l, zc, mod_ctx, w_ff1, w_ff2, ln1, ln2, tm=tm_ctx)

    return x
```

```python
import functools

import jax
import jax.numpy as jnp
from jax import lax
from jax.experimental import pallas as pl
from jax.experimental.pallas import tpu as pltpu

D_MODEL = 1024
DEPTH = 4
GRID_W = 64
RET_HEADS = 8
HEAD_DIM = 64
RET_WIDTH = RET_HEADS * HEAD_DIM
CONV_CH = 512
CONV_WIDTH = 31
CONV_HALO = 16
IN_WIDTH = 3072
D_FF = 4 * D_MODEL
CHUNK = 128
ROPE_BASE = 10000.0
LN_EPS = 1e-5
ALPHA = (2 * DEPTH) ** 0.25
LANES = 128
N_PAIRS = RET_HEADS // 2
FF_BLOCK = 1024
ROW_GROUP = 512
MLP_ROWS = 512
N_SLABS = CONV_CH // LANES
HALF = CHUNK // 2
VMEM_LIMIT = 56 * 1024 * 1024

F32 = jnp.float32
BF16 = jnp.bfloat16


def _resident(shape):
    zeros = (0,) * len(shape)
    return pl.BlockSpec(shape, lambda *_: zeros, pipeline_mode=pl.Buffered(1))


def _layer_resident(layer, shape):
    zeros = (0,) * len(shape)
    return pl.BlockSpec((None,) + tuple(shape), lambda *_: (layer,) + zeros, pipeline_mode=pl.Buffered(1))


def _params():
    return pltpu.CompilerParams(dimension_semantics=("arbitrary", "arbitrary"),
                                vmem_limit_bytes=VMEM_LIMIT)


def _layer_norm(x, w, b):
    mu = jnp.mean(x, axis=-1, keepdims=True)
    d = x - mu
    var = jnp.mean(d * d, axis=-1, keepdims=True)
    return d * lax.rsqrt(var + LN_EPS) * w + b


def _pair_masks(rows):
    lane = lax.broadcasted_iota(jnp.int32, (rows, LANES), 1)
    return lane < HEAD_DIM, lane >= HEAD_DIM


def _block_diag_mask():
    r = lax.broadcasted_iota(jnp.int32, (LANES, LANES), 0)
    c = lax.broadcasted_iota(jnp.int32, (LANES, LANES), 1)
    return ((r < HEAD_DIM) == (c < HEAD_DIM)).astype(F32)


def _adaln_kernel(cond_ref, w_ref, b_ref, o_ref):
    cond = cond_ref[...]
    act = cond * jax.nn.sigmoid(cond)
    o_ref[0] = jnp.dot(act, w_ref[0], preferred_element_type=F32,
                       precision=lax.Precision.HIGHEST) + b_ref[0]


def _adaln(cond, w_ada, b_ada):
    width = w_ada.shape[-1]
    block = 1536
    return pl.pallas_call(
        _adaln_kernel,
        grid=(DEPTH, width // block),
        in_specs=[pl.BlockSpec((8, D_MODEL), lambda l, j: (0, 0)),
                  pl.BlockSpec((1, D_MODEL, block), lambda l, j: (l, 0, j)),
                  pl.BlockSpec((1, 1, block), lambda l, j: (l, 0, j))],
        out_specs=pl.BlockSpec((1, 8, block), lambda l, j: (l, 0, j)),
        out_shape=jax.ShapeDtypeStruct((DEPTH, 8, width), F32),
        compiler_params=_params(),
        name="adaln",
    )(cond, w_ada, b_ada.reshape(DEPTH, 1, width))


def _inproj_kernel(x_ref, mod_ref, cos_ref, sin_ref, w_ref, rf_ref, rb_ref, kdf_ref, kdb_ref, cdb_ref,
                   kfin_ref, sb0_ref,
                   q_ref, qf_ref, qb_ref, kk_ref, kfs_ref, vv_ref, sg_ref, u_ref, sbc_ref, sbfin_ref, sffin_ref,
                   kb_scr, kfin_scr, v_scr, sb_scr, sf_scr, *, tm, k_scale, want_final_fwd):
    t = pl.program_id(1)
    nt = pl.num_programs(1)
    cpt = tm // CHUNK

    @pl.when(t == 0)
    def _():
        sb_scr[...] = sb0_ref[0]
        sf_scr[...] = jnp.zeros_like(sf_scr)

    shift = mod_ref[0, 0:1, :]
    scale = mod_ref[0, 1:2, :]
    h = (x_ref[0] * (1.0 + scale) + shift).astype(BF16)

    def proj(col):
        return jnp.dot(h, w_ref[:, col:col + RET_WIDTH], preferred_element_type=F32)

    cos = cos_ref[...]
    sin = sin_ref[...]
    even = (lax.broadcasted_iota(jnp.int32, (tm, LANES), 1) % 2) == 0

    def rope(p):
        swapped = jnp.where(even, pltpu.roll(p, LANES - 1, 1), pltpu.roll(p, 1, 1))
        return p * cos + swapped * sin

    def put_stacked(ref, j, t):
        left, right = _pair_masks(CHUNK)
        for c in range(cpt):
            tc = t[c * CHUNK:(c + 1) * CHUNK]
            ref[0, j, 2 * c * CHUNK:(2 * c + 1) * CHUNK, :] = jnp.where(left, tc, 0.0).astype(BF16)
            ref[0, j, (2 * c + 1) * CHUNK:(2 * c + 2) * CHUNK, :] = jnp.where(right, tc, 0.0).astype(BF16)

    pq = proj(0)
    pk = proj(RET_WIDTH)
    for j in range(N_PAIRS):
        lanes = slice(j * LANES, (j + 1) * LANES)
        qq = rope(pq[:, lanes])
        q_ref[0, j] = qq.astype(BF16)
        qf_ref[0, j] = (qq * rf_ref[:, lanes]).astype(BF16)
        qb_ref[0, j] = (qq * rb_ref[:, lanes]).astype(BF16)
        kk = rope(pk[:, lanes]) * k_scale
        put_stacked(kk_ref, j, kk)
        put_stacked(kfs_ref, j, kk * kdf_ref[:, lanes])
        kb_scr[:, lanes] = (kk * kdb_ref[:, lanes]).astype(BF16)
        if want_final_fwd:
            kfin_scr[:, lanes] = (kk * kfin_ref[:, lanes]).astype(BF16)

    pv = proj(2 * RET_WIDTH)
    v_scr[...] = pv.astype(BF16)
    g = proj(3 * RET_WIDTH)
    sg = (g * jax.nn.sigmoid(g)).astype(BF16)
    for j in range(N_PAIRS):
        lanes = slice(j * LANES, (j + 1) * LANES)
        put_stacked(vv_ref, j, pv[:, lanes])
        sg_ref[0, j] = sg[:, lanes]
    a = proj(4 * RET_WIDTH)
    gate = proj(5 * RET_WIDTH)
    u_ref[0] = (a * jax.nn.sigmoid(gate)).astype(BF16)

    bd = _block_diag_mask()
    contract_rows = (((0,), (0,)), ((), ()))
    for c in reversed(range(cpt)):
        rows = slice(c * CHUNK, (c + 1) * CHUNK)
        for j in range(N_PAIRS):
            lanes = slice(j * LANES, (j + 1) * LANES)
            vp = v_scr[rows, lanes]
            s = sb_scr[j]
            sbc_ref[0, c, j] = s.astype(BF16)
            kv = lax.dot_general(kb_scr[rows, lanes], vp, contract_rows, preferred_element_type=F32)
            sb_scr[j] = s * cdb_ref[:, lanes] + kv * bd
            if want_final_fwd:
                kvf = lax.dot_general(kfin_scr[rows, lanes], vp, contract_rows,
                                      preferred_element_type=F32)
                sf_scr[j] = sf_scr[j] + kvf * bd

    @pl.when(t == nt - 1)
    def _():
        sbfin_ref[0] = sb_scr[...]
        sffin_ref[0] = sf_scr[...]


def _inproj(layer, x, mod, cos_t, sin_t, w_in, tabs, kfin, sb0, *, tm, k_scale):
    b, n, _ = x.shape
    nt = n // tm
    cpt = tm // CHUNK
    rev = lambda bb, t: (bb, nt - 1 - t, 0)
    rev_tab = lambda bb, t: (nt - 1 - t, 0)
    seq = jax.ShapeDtypeStruct((b, N_PAIRS, n, LANES), BF16)
    state = jax.ShapeDtypeStruct((b, N_PAIRS, LANES, LANES), F32)
    seq_spec = pl.BlockSpec((1, N_PAIRS, tm, LANES), lambda bb, t: (bb, 0, nt - 1 - t, 0))
    stk = jax.ShapeDtypeStruct((b, N_PAIRS, 2 * n, LANES), BF16)
    stk_spec = pl.BlockSpec((1, N_PAIRS, 2 * tm, LANES), lambda bb, t: (bb, 0, nt - 1 - t, 0))
    state_spec = pl.BlockSpec((1, N_PAIRS, LANES, LANES), lambda bb, t: (bb, 0, 0, 0))
    want_final_fwd = kfin is not None
    if want_final_fwd:
        kfin_spec = pl.BlockSpec((tm, RET_WIDTH), rev_tab)
        kfin_rows = tm
    else:
        kfin = jnp.zeros((8, RET_WIDTH), F32)
        kfin_spec = _resident((8, RET_WIDTH))
        kfin_rows = 16
    return pl.pallas_call(
        functools.partial(_inproj_kernel, tm=tm, k_scale=k_scale, want_final_fwd=want_final_fwd),
        grid=(b, nt),
        in_specs=[pl.BlockSpec((1, tm, D_MODEL), rev),
                  pl.BlockSpec((1, 8, D_MODEL), lambda bb, t: (bb, 0, 0)),
                  pl.BlockSpec((tm, LANES), rev_tab),
                  pl.BlockSpec((tm, LANES), rev_tab),
                  _layer_resident(layer, (D_MODEL, IN_WIDTH)),
                  _resident((tm, RET_WIDTH)),
                  _resident((tm, RET_WIDTH)),
                  _resident((tm, RET_WIDTH)),
                  _resident((tm, RET_WIDTH)),
                  _resident((1, RET_WIDTH)),
                  kfin_spec,
                  state_spec],
        out_specs=[seq_spec] * 3 + [stk_spec] * 3 + [seq_spec, pl.BlockSpec((1, tm, CONV_CH), rev),
                   pl.BlockSpec((1, cpt, N_PAIRS, LANES, LANES),
                                lambda bb, t: (bb, nt - 1 - t, 0, 0, 0)),
                   state_spec, state_spec],
        out_shape=[seq] * 3 + [stk] * 3 + [seq, jax.ShapeDtypeStruct((b, n, CONV_CH), BF16),
                   jax.ShapeDtypeStruct((b, n // CHUNK, N_PAIRS, LANES, LANES), BF16),
                   state, state],
        scratch_shapes=[pltpu.VMEM((tm, RET_WIDTH), BF16),
                        pltpu.VMEM((kfin_rows, RET_WIDTH), BF16),
                        pltpu.VMEM((tm, RET_WIDTH), BF16),
                        pltpu.VMEM((N_PAIRS, LANES, LANES), F32),
                        pltpu.VMEM((N_PAIRS, LANES, LANES), F32)],
        compiler_params=_params(),
        name="inproj",
    )(x, mod, cos_t, sin_t, w_in, tabs["rf"], tabs["rb"], tabs["kdf"], tabs["kdb"], tabs["cdb"], kfin, sb0)


def _mixer_kernel(q_ref, qf_ref, qb_ref, kk_ref, kfs_ref, vv_ref, sg_ref, u_ref, up_ref, un_ref, sbc_ref,
                  sf0_ref, x_ref, mod_ref, mask_ref, cdf_ref, gn_ref, cw_ref, cv_ref, ones_ref, wout_ref, ln_ref,
                  o_ref, sffin_ref,
                  sf_scr, sfb_scr, pr_scr, o_scr, uf_scr, y_scr, mix_scr, *, tm):
    t = pl.program_id(1)
    nt = pl.num_programs(1)
    cpt = tm // CHUNK
    n_iter = cpt * N_PAIRS // 2

    @pl.when(t == 0)
    def _():
        sf_scr[...] = sf0_ref[0]
        sfb_scr[...] = sf0_ref[0].astype(BF16)

    contract_rows = (((0,), (0,)), ((), ()))
    contract_lanes = (((1,), (1,)), ((), ()))

    def unit_of(it, e):
        c = it % cpt
        return (2 * (it // cpt) + e, pl.ds(pl.multiple_of(c * CHUNK, CHUNK), CHUNK),
                pl.ds(pl.multiple_of(2 * c * CHUNK, 2 * CHUNK), 2 * CHUNK))

    def probs_into(slot, it):
        for e in range(2):
            j, rows, srows = unit_of(it, e)
            sc = lax.dot_general(q_ref[0, j, rows, :], kk_ref[0, j, srows, :], contract_lanes,
                                 preferred_element_type=F32)
            pr_scr[slot, e] = (sc * mask_ref[j]).astype(BF16)

    def conv_unit(idx):
        slab = idx % N_SLABS
        base = (idx // N_SLABS) * CHUNK
        first = CONV_HALO - CONV_WIDTH // 2
        bias = jnp.broadcast_to(cw_ref[slab, CONV_WIDTH:CONV_WIDTH + 1, :], (HALF, LANES))
        accs = [[bias, None] for _ in range(2)]
        for tap in range(CONV_WIDTH):
            w = cw_ref[slab, tap:tap + 1, :]
            for par in range(2):
                term = w * uf_scr[slab, pl.ds(base + par + first + tap, HALF, stride=2), :]
                prev = accs[par][tap % 2]
                accs[par][tap % 2] = term if prev is None else prev + term
        for par in range(2):
            y_scr[slab, pl.ds(base + par, HALF, stride=2), :] = accs[par][0] + accs[par][1]

    probs_into(0, 0)

    for j in range(N_SLABS):
        lanes = slice(j * LANES, (j + 1) * LANES)
        uf_scr[j, 0:CONV_HALO, :] = jnp.where(t > 0, up_ref[0, :, lanes].astype(F32), 0.0)
        uf_scr[j, CONV_HALO:CONV_HALO + tm, :] = u_ref[0, :, lanes].astype(F32)
        uf_scr[j, CONV_HALO + tm:, :] = jnp.where(t < nt - 1, un_ref[0, :, lanes].astype(F32), 0.0)

    def step(it, carry):
        c = it % cpt
        units = [unit_of(it, e) for e in range(2)]
        probs = [pr_scr[it % 2, e] for e in range(2)]
        states = [sf_scr[j] for j, _, _ in units]
        states_b = [sfb_scr[j] for j, _, _ in units]
        new_states = []
        for e, (j, rows, srows) in enumerate(units):
            vv = vv_ref[0, j, srows, :]
            lhs = jnp.concatenate([probs[e], qf_ref[0, j, rows, :], qb_ref[0, j, rows, :]], axis=1)
            rhs = jnp.concatenate([vv, states_b[e], sbc_ref[0, c, j]], axis=0)
            o_scr[j, rows, :] = jnp.dot(lhs, rhs, preferred_element_type=F32)
            kv = lax.dot_general(kfs_ref[0, j, srows, :], vv, contract_rows, preferred_element_type=F32)
            new_states.append(states[e] * cdf_ref[j] + kv)
        probs_into((it + 1) % 2, jnp.minimum(it + 1, n_iter - 1))
        for e, (j, _, _) in enumerate(units):
            sf_scr[j] = new_states[e]
            sfb_scr[j] = new_states[e].astype(BF16)
        conv_unit(2 * it)
        conv_unit(2 * it + 1)
        return carry

    lax.fori_loop(0, n_iter, step, 0)

    gate = mod_ref[0, 2:3, :]
    group = min(tm, ROW_GROUP)
    for g in range(tm // group):
        rows = slice(g * group, (g + 1) * group)
        for i in range(2):
            lanes = slice(i * 2 * LANES, (i + 1) * 2 * LANES)
            o = jnp.concatenate([o_scr[2 * i, rows, :], o_scr[2 * i + 1, rows, :]], axis=1)
            d = o - jnp.dot(o.astype(BF16), ones_ref[...], preferred_element_type=F32)
            var = jnp.dot((d * d).astype(BF16), ones_ref[...], preferred_element_type=F32)
            on = d * lax.rsqrt(var + LN_EPS) * gn_ref[0:1, lanes] + gn_ref[1:2, lanes]
            sg = jnp.concatenate([sg_ref[0, 2 * i, rows, :], sg_ref[0, 2 * i + 1, rows, :]], axis=1).astype(F32)
            mix_scr[rows, lanes] = (sg * on).astype(BF16)
        y = _layer_norm(jnp.concatenate([y_scr[j, rows, :] for j in range(N_SLABS)], axis=1),
                        cv_ref[0:1, :], cv_ref[1:2, :])
        mix_scr[rows, RET_WIDTH:] = (y * jax.nn.sigmoid(y)).astype(BF16)
        mix = jnp.dot(mix_scr[rows, :], wout_ref[...], preferred_element_type=F32)
        o_ref[0, rows, :] = _layer_norm(ALPHA * x_ref[0, rows, :] + gate * mix, ln_ref[0:1, :], ln_ref[1:2, :])

    @pl.when(t == nt - 1)
    def _():
        sffin_ref[0] = sf_scr[...]


def _group_ones():
    g = jnp.arange(2 * LANES) // HEAD_DIM
    return jnp.where(g[:, None] == g[None, :], 1.0 / HEAD_DIM, 0.0).astype(BF16)


def _mixer(layer, q, qf, qb, kk, kfs, vv, sg, u, sbc, sf0, x, mod, tabs, gn, cw, cv, w_out, ln1, *, tm):
    b, n, _ = x.shape
    nt = n // tm
    cpt = tm // CHUNK
    hpt = tm // CONV_HALO
    last_halo = n // CONV_HALO - 1
    tile = lambda bb, t: (bb, t, 0)
    seq_spec = pl.BlockSpec((1, N_PAIRS, tm, LANES), lambda bb, t: (bb, 0, t, 0))
    stk_spec = pl.BlockSpec((1, N_PAIRS, 2 * tm, LANES), lambda bb, t: (bb, 0, t, 0))
    state_spec = pl.BlockSpec((1, N_PAIRS, LANES, LANES), lambda bb, t: (bb, 0, 0, 0))
    ones = _group_ones()
    return pl.pallas_call(
        functools.partial(_mixer_kernel, tm=tm),
        grid=(b, nt),
        in_specs=[seq_spec] * 3 + [stk_spec] * 3 + [seq_spec,
                  pl.BlockSpec((1, tm, CONV_CH), tile),
                  pl.BlockSpec((1, CONV_HALO, CONV_CH),
                               lambda bb, t: (bb, jnp.maximum(t * hpt - 1, 0), 0)),
                  pl.BlockSpec((1, CONV_HALO, CONV_CH),
                               lambda bb, t: (bb, jnp.minimum((t + 1) * hpt, last_halo), 0)),
                  pl.BlockSpec((1, cpt, N_PAIRS, LANES, LANES), lambda bb, t: (bb, t, 0, 0, 0)),
                  state_spec,
                  pl.BlockSpec((1, tm, D_MODEL), tile),
                  pl.BlockSpec((1, 8, D_MODEL), lambda bb, t: (bb, 0, 0)),
                  _resident((N_PAIRS, CHUNK, 2 * CHUNK)),
                  _resident((N_PAIRS, 1, LANES)),
                  _resident((2, RET_WIDTH)),
                  _resident((N_SLABS, 32, LANES)),
                  _resident((8, CONV_CH)),
                  _resident(ones.shape),
                  _layer_resident(layer, (D_MODEL, D_MODEL)),
                  _resident((2, D_MODEL))],
        out_specs=[pl.BlockSpec((1, tm, D_MODEL), tile), state_spec],
        out_shape=[jax.ShapeDtypeStruct((b, n, D_MODEL), F32),
                   jax.ShapeDtypeStruct((b, N_PAIRS, LANES, LANES), F32)],
        scratch_shapes=[pltpu.VMEM((N_PAIRS, LANES, LANES), F32),
                        pltpu.VMEM((N_PAIRS, LANES, LANES), BF16),
                        pltpu.VMEM((2, 2, CHUNK, 2 * CHUNK), BF16),
                        pltpu.VMEM((N_PAIRS, tm, LANES), F32),
                        pltpu.VMEM((N_SLABS, tm + 2 * CONV_HALO, LANES), F32),
                        pltpu.VMEM((N_SLABS, tm, LANES), F32),
                        pltpu.VMEM((tm, 2 * RET_WIDTH), BF16)],
        compiler_params=_params(),
        name="mixer",
    )(q, qf, qb, kk, kfs, vv, sg, u, u, u, sbc, sf0, x, mod, tabs["mask"],
      tabs["cdf"].reshape(N_PAIRS, 1, LANES), gn, cw, cv, ones, w_out, ln1)


def _mlp_kernel(x_ref, mod_ref, w1_ref, w2_ref, ln_ref, o_ref, *, sub):
    for i in range(x_ref.shape[1] // sub):
        rows = slice(i * sub, (i + 1) * sub)
        x = x_ref[0, rows, :]
        h = (x * (1.0 + mod_ref[0, 4:5, :]) + mod_ref[0, 3:4, :]).astype(BF16)
        y = jnp.zeros(x.shape, F32)
        for j in range(D_FF // FF_BLOCK):
            cols = slice(j * FF_BLOCK, (j + 1) * FF_BLOCK)
            a = jnp.maximum(jnp.dot(h, w1_ref[:, cols], preferred_element_type=F32), 0.0)
            y = y + jnp.dot((a * a).astype(BF16), w2_ref[cols, :], preferred_element_type=F32)
        o_ref[0, rows, :] = _layer_norm(ALPHA * x + mod_ref[0, 5:6, :] * y, ln_ref[0:1, :], ln_ref[1:2, :])


def _mlp(layer, x, mod, w1, w2, ln2, *, tm):
    b, n, _ = x.shape
    tile = lambda bb, t: (bb, t, 0)
    return pl.pallas_call(
        functools.partial(_mlp_kernel, sub=min(tm, MLP_ROWS)),
        grid=(b, n // tm),
        in_specs=[pl.BlockSpec((1, tm, D_MODEL), tile),
                  pl.BlockSpec((1, 8, D_MODEL), lambda bb, t: (bb, 0, 0)),
                  _layer_resident(layer, (D_MODEL, D_FF)),
                  _layer_resident(layer, (D_FF, D_MODEL)),
                  _resident((2, D_MODEL))],
        out_specs=pl.BlockSpec((1, tm, D_MODEL), tile),
        out_shape=jax.ShapeDtypeStruct((b, n, D_MODEL), F32),
        compiler_params=_params(),
        name="mlp",
    )(x, mod, w1, w2, ln2)


def _rope_tables(n):
    rows = n // GRID_W
    row = jnp.repeat(jnp.arange(rows, dtype=F32), GRID_W)
    col = jnp.tile(jnp.arange(GRID_W, dtype=F32), rows)
    n_freq = HEAD_DIM // 4
    inv = ROPE_BASE ** (-jnp.arange(n_freq, dtype=F32) / n_freq)
    ang = jnp.concatenate([row[:, None] * inv, col[:, None] * inv], axis=-1)
    cos = jnp.repeat(jnp.cos(ang), 2, axis=-1)
    sin = jnp.repeat(jnp.sin(ang), 2, axis=-1) * jnp.tile(jnp.array([-1.0, 1.0], F32), HEAD_DIM // 2)
    return jnp.tile(cos, (1, 2)), jnp.tile(sin, (1, 2))


def _per_lane(t):
    return jnp.repeat(t.T, HEAD_DIM, axis=1)


def _decay_tables(lg_f, lg_b, tm):
    idx = jnp.arange(CHUNK, dtype=F32)
    lf = lg_f.astype(F32)[:, None]
    lb = lg_b.astype(F32)[:, None]
    diff = idx[:, None] - idx[None, :]
    fwd = jnp.where(diff >= 0, jnp.exp(lf[:, :, None] * jnp.maximum(diff, 0.0)), 0.0)
    bwd = jnp.where(diff <= 0, jnp.exp(lb[:, :, None] * jnp.maximum(-diff, 0.0)), 0.0)
    mask = (fwd + bwd).reshape(N_PAIRS, 2, CHUNK, CHUNK).transpose(0, 2, 1, 3)
    reps = tm // CHUNK
    return {
        "mask": mask.reshape(N_PAIRS, CHUNK, 2 * CHUNK),
        "rf": jnp.tile(_per_lane(jnp.exp(lf * (idx + 1.0))), (reps, 1)),
        "rb": jnp.tile(_per_lane(jnp.exp(lb * (CHUNK - idx))), (reps, 1)),
        "kdf": jnp.tile(_per_lane(jnp.exp(lf * (CHUNK - 1.0 - idx))), (reps, 1)),
        "kdb": jnp.tile(_per_lane(jnp.exp(lb * idx)), (reps, 1)),
        "cdf": _per_lane(jnp.exp(lf * CHUNK)),
        "cdb": _per_lane(jnp.exp(lb * CHUNK)),
    }


def _final_weights(lg_f, n):
    pos = jnp.arange(n, dtype=F32)
    return _per_lane(jnp.exp(lg_f.astype(F32)[:, None] * (n - 1.0 - pos)))


def _rows8(*rows):
    out = jnp.stack(rows, axis=-2)
    pad = [(0, 0)] * (out.ndim - 2) + [(0, 8 - len(rows)), (0, 0)]
    return jnp.pad(out, pad)


def kernel(x, c, ctx, c_ctx, w_ada, b_ada, w_in, ret_log_rate_fwd, ret_log_rate_bwd, ret_gn_w, ret_gn_b,
           conv_w, conv_b, conv_ln_w, conv_ln_b, w_out, ln1_w, ln1_b, w_ff1, w_ff2, ln2_w, ln2_b):
    b, n, _ = x.shape
    n_ctx = ctx.shape[1]
    tm_lat, tm_ctx, tm_mlp = 512, n_ctx, 1024
    w_in, w_out, w_ff1, w_ff2 = (w.astype(BF16) for w in (w_in, w_out, w_ff1, w_ff2))

    cond = jnp.zeros((8, D_MODEL), F32).at[:b].set(c).at[b].set(c_ctx)
    ada = _adaln(cond, w_ada, b_ada).reshape(DEPTH, 8, 6, D_MODEL)

    cos_lat, sin_lat = _rope_tables(n)
    cos_ctx = jnp.ones((n_ctx, LANES), F32)
    sin_ctx = jnp.zeros((n_ctx, LANES), F32)
    zero_state = jnp.zeros((b, N_PAIRS, LANES, LANES), F32)

    for l in range(DEPTH):
        last = l == DEPTH - 1
        mod_lat = jnp.pad(ada[l, :b], ((0, 0), (0, 2), (0, 0)))
        mod_ctx = jnp.broadcast_to(jnp.pad(ada[l, b], ((0, 2), (0, 0))), (b, 8, D_MODEL))
        lg_f = -jnp.exp(ret_log_rate_fwd[l])
        lg_b = -jnp.exp(ret_log_rate_bwd[l])
        tabs_lat = _decay_tables(lg_f, lg_b, tm_lat)
        tabs_ctx = _decay_tables(lg_f, lg_b, tm_ctx)
        gn = jnp.stack([ret_gn_w[l], ret_gn_b[l]])
        cw = jnp.concatenate([conv_w[l], conv_b[l][None]], axis=0).reshape(32, N_SLABS, LANES).transpose(1, 0, 2)
        cv = _rows8(conv_ln_w[l], conv_ln_b[l])
        ln1 = jnp.stack([ln1_w[l], ln1_b[l]])
        ln2 = jnp.stack([ln2_w[l], ln2_b[l]])

        k_scale_ctx = 1.0 if last else HEAD_DIM ** -0.5
        (qc, qfc, qbc, kkc, kfsc, vvc, sgc, uc, sbc_c, sb_fin, sf_fin) = _inproj(
            l, ctx, mod_ctx, cos_ctx, sin_ctx, w_in, tabs_ctx, _final_weights(lg_f, n_ctx), zero_state,
            tm=tm_ctx, k_scale=k_scale_ctx)

        (q, qf, qb, kk, kfs, vv, sg, u, sbc, _, _) = _inproj(
            l, x, mod_lat, cos_lat, sin_lat, w_in, tabs_lat, None, sb_fin,
            tm=tm_lat, k_scale=HEAD_DIM ** -0.5)
        x, _ = _mixer(l, q, qf, qb, kk, kfs, vv, sg, u, sbc, sf_fin, x, mod_lat, tabs_lat, gn, cw, cv, w_out, ln1,
                      tm=tm_lat)
        x = _mlp(l, x, mod_lat, w_ff1, w_ff2, ln2, tm=tm_mlp)

        if not last:
            ctx, _ = _mixer(l, qc, qfc, qbc, kkc, kfsc, vvc, sgc, uc, sbc_c, zero_state, ctx, mod_ctx, tabs_ctx, gn,
                            cw, cv, w_out, ln1, tm=tm_ctx)
            ctx = _mlp(l, ctx, mod_ctx, w_ff1, w_ff2, ln2, tm=tm_ctx)

    return x
```

```python
import functools

import jax
import jax.numpy as jnp
from jax import lax
from jax.experimental import pallas as pl
from jax.experimental.pallas import tpu as pltpu

D_MODEL = 1024
DEPTH = 4
GRID_W = 64
RET_HEADS = 8
HEAD_DIM = 64
RET_WIDTH = RET_HEADS * HEAD_DIM
CONV_CH = 512
CONV_WIDTH = 31
CONV_HALO = 16
IN_WIDTH = 3072
D_FF = 4 * D_MODEL
CHUNK = 128
ROPE_BASE = 10000.0
LN_EPS = 1e-5
ALPHA = (2 * DEPTH) ** 0.25
LANES = 128
N_PAIRS = RET_HEADS // 2
FF_BLOCK = 1024
ROW_GROUP = 512
MLP_ROWS = 256
N_SLABS = CONV_CH // LANES
HALF = CHUNK // 2
VMEM_LIMIT = 56 * 1024 * 1024

F32 = jnp.float32
BF16 = jnp.bfloat16


def _resident(shape):
    zeros = (0,) * len(shape)
    return pl.BlockSpec(shape, lambda *_: zeros, pipeline_mode=pl.Buffered(1))


def _layer_resident(layer, shape):
    zeros = (0,) * len(shape)
    return pl.BlockSpec((None,) + tuple(shape), lambda *_: (layer,) + zeros, pipeline_mode=pl.Buffered(1))


def _params():
    return pltpu.CompilerParams(dimension_semantics=("arbitrary", "arbitrary"),
                                vmem_limit_bytes=VMEM_LIMIT)


def _layer_norm(x, w, b):
    mu = jnp.mean(x, axis=-1, keepdims=True)
    d = x - mu
    var = jnp.mean(d * d, axis=-1, keepdims=True)
    return d * lax.rsqrt(var + LN_EPS) * w + b


def _pair_masks(rows):
    lane = lax.broadcasted_iota(jnp.int32, (rows, LANES), 1)
    return lane < HEAD_DIM, lane >= HEAD_DIM


def _block_diag_mask():
    r = lax.broadcasted_iota(jnp.int32, (LANES, LANES), 0)
    c = lax.broadcasted_iota(jnp.int32, (LANES, LANES), 1)
    return ((r < HEAD_DIM) == (c < HEAD_DIM)).astype(F32)


def _adaln_kernel(cond_ref, w_ref, b_ref, o_ref):
    cond = cond_ref[...]
    act = cond * jax.nn.sigmoid(cond)
    o_ref[0] = jnp.dot(act, w_ref[0], preferred_element_type=F32,
                       precision=lax.Precision.HIGHEST) + b_ref[0]


def _adaln(cond, w_ada, b_ada):
    width = w_ada.shape[-1]
    block = 1536
    return pl.pallas_call(
        _adaln_kernel,
        grid=(DEPTH, width // block),
        in_specs=[pl.BlockSpec((8, D_MODEL), lambda l, j: (0, 0)),
                  pl.BlockSpec((1, D_MODEL, block), lambda l, j: (l, 0, j)),
                  pl.BlockSpec((1, 1, block), lambda l, j: (l, 0, j))],
        out_specs=pl.BlockSpec((1, 8, block), lambda l, j: (l, 0, j)),
        out_shape=jax.ShapeDtypeStruct((DEPTH, 8, width), F32),
        compiler_params=_params(),
        name="adaln",
    )(cond, w_ada, b_ada.reshape(DEPTH, 1, width))


def _inproj_kernel(x_ref, mod_ref, cos_ref, sin_ref, w_ref, rf_ref, rb_ref, kdf_ref, kdb_ref, cdb_ref,
                   kfin_ref, sb0_ref,
                   q_ref, qf_ref, qb_ref, kk_ref, kfs_ref, vv_ref, sg_ref, u_ref, sbc_ref, sbfin_ref, sffin_ref,
                   kb_scr, kfin_scr, v_scr, sb_scr, sf_scr, *, tm, k_scale, want_final_fwd):
    t = pl.program_id(1)
    nt = pl.num_programs(1)
    cpt = tm // CHUNK

    @pl.when(t == 0)
    def _():
        sb_scr[...] = sb0_ref[0]
        sf_scr[...] = jnp.zeros_like(sf_scr)

    shift = mod_ref[0, 0:1, :]
    scale = mod_ref[0, 1:2, :]
    h = (x_ref[0] * (1.0 + scale) + shift).astype(BF16)

    def proj(col):
        return jnp.dot(h, w_ref[:, col:col + RET_WIDTH], preferred_element_type=F32)

    cos = cos_ref[...]
    sin = sin_ref[...]
    even = (lax.broadcasted_iota(jnp.int32, (tm, LANES), 1) % 2) == 0

    def rope(p):
        swapped = jnp.where(even, pltpu.roll(p, LANES - 1, 1), pltpu.roll(p, 1, 1))
        return p * cos + swapped * sin

    def put_stacked(ref, j, t):
        left, right = _pair_masks(CHUNK)
        for c in range(cpt):
            tc = t[c * CHUNK:(c + 1) * CHUNK]
            ref[0, j, 2 * c * CHUNK:(2 * c + 1) * CHUNK, :] = jnp.where(left, tc, 0.0).astype(BF16)
            ref[0, j, (2 * c + 1) * CHUNK:(2 * c + 2) * CHUNK, :] = jnp.where(right, tc, 0.0).astype(BF16)

    pq = proj(0)
    pk = proj(RET_WIDTH)
    for j in range(N_PAIRS):
        lanes = slice(j * LANES, (j + 1) * LANES)
        qq = rope(pq[:, lanes])
        q_ref[0, j] = qq.astype(BF16)
        qf_ref[0, j] = (qq * rf_ref[:, lanes]).astype(BF16)
        qb_ref[0, j] = (qq * rb_ref[:, lanes]).astype(BF16)
        kk = rope(pk[:, lanes]) * k_scale
        put_stacked(kk_ref, j, kk)
        put_stacked(kfs_ref, j, kk * kdf_ref[:, lanes])
        kb_scr[:, lanes] = (kk * kdb_ref[:, lanes]).astype(BF16)
        if want_final_fwd:
            kfin_scr[:, lanes] = (kk * kfin_ref[:, lanes]).astype(BF16)

    pv = proj(2 * RET_WIDTH)
    v_scr[...] = pv.astype(BF16)
    g = proj(3 * RET_WIDTH)
    sg = (g * jax.nn.sigmoid(g)).astype(BF16)
    for j in range(N_PAIRS):
        lanes = slice(j * LANES, (j + 1) * LANES)
        put_stacked(vv_ref, j, pv[:, lanes])
        sg_ref[0, j] = sg[:, lanes]
    a = proj(4 * RET_WIDTH)
    gate = proj(5 * RET_WIDTH)
    u_ref[0] = (a * jax.nn.sigmoid(gate)).astype(BF16)

    bd = _block_diag_mask()
    contract_rows = (((0,), (0,)), ((), ()))
    for c in reversed(range(cpt)):
        rows = slice(c * CHUNK, (c + 1) * CHUNK)
        for j in range(N_PAIRS):
            lanes = slice(j * LANES, (j + 1) * LANES)
            vp = v_scr[rows, lanes]
            s = sb_scr[j]
            sbc_ref[0, c, j] = s.astype(BF16)
            kv = lax.dot_general(kb_scr[rows, lanes], vp, contract_rows, preferred_element_type=F32)
            sb_scr[j] = s * cdb_ref[:, lanes] + kv * bd
            if want_final_fwd:
                kvf = lax.dot_general(kfin_scr[rows, lanes], vp, contract_rows,
                                      preferred_element_type=F32)
                sf_scr[j] = sf_scr[j] + kvf * bd

    @pl.when(t == nt - 1)
    def _():
        sbfin_ref[0] = sb_scr[...]
        sffin_ref[0] = sf_scr[...]


def _inproj(layer, x, mod, cos_t, sin_t, w_in, tabs, kfin, sb0, *, tm, k_scale):
    b, n, _ = x.shape
    nt = n // tm
    cpt = tm // CHUNK
    rev = lambda bb, t: (bb, nt - 1 - t, 0)
    rev_tab = lambda bb, t: (nt - 1 - t, 0)
    seq = jax.ShapeDtypeStruct((b, N_PAIRS, n, LANES), BF16)
    state = jax.ShapeDtypeStruct((b, N_PAIRS, LANES, LANES), F32)
    seq_spec = pl.BlockSpec((1, N_PAIRS, tm, LANES), lambda bb, t: (bb, 0, nt - 1 - t, 0))
    stk = jax.ShapeDtypeStruct((b, N_PAIRS, 2 * n, LANES), BF16)
    stk_spec = pl.BlockSpec((1, N_PAIRS, 2 * tm, LANES), lambda bb, t: (bb, 0, nt - 1 - t, 0))
    state_spec = pl.BlockSpec((1, N_PAIRS, LANES, LANES), lambda bb, t: (bb, 0, 0, 0))
    mod_spec = pl.BlockSpec((None, 1, 8, D_MODEL), lambda bb, t: (layer, bb if mod.shape[1] > 1 else 0, 0, 0))
    want_final_fwd = kfin is not None
    if want_final_fwd:
        kfin_spec = pl.BlockSpec((None, tm, RET_WIDTH), lambda bb, t: (layer, nt - 1 - t, 0))
        kfin_rows = tm
    else:
        kfin = jnp.zeros((8, RET_WIDTH), F32)
        kfin_spec = _resident((8, RET_WIDTH))
        kfin_rows = 16
    return pl.pallas_call(
        functools.partial(_inproj_kernel, tm=tm, k_scale=k_scale, want_final_fwd=want_final_fwd),
        grid=(b, nt),
        in_specs=[pl.BlockSpec((1, tm, D_MODEL), rev),
                  mod_spec,
                  pl.BlockSpec((tm, LANES), rev_tab),
                  pl.BlockSpec((tm, LANES), rev_tab),
                  _layer_resident(layer, (D_MODEL, IN_WIDTH)),
                  _layer_resident(layer, (tm, RET_WIDTH)),
                  _layer_resident(layer, (tm, RET_WIDTH)),
                  _layer_resident(layer, (tm, RET_WIDTH)),
                  _layer_resident(layer, (tm, RET_WIDTH)),
                  _layer_resident(layer, (1, RET_WIDTH)),
                  kfin_spec,
                  state_spec],
        out_specs=[seq_spec] * 3 + [stk_spec] * 3 + [seq_spec, pl.BlockSpec((1, tm, CONV_CH), rev),
                   pl.BlockSpec((1, cpt, N_PAIRS, LANES, LANES),
                                lambda bb, t: (bb, nt - 1 - t, 0, 0, 0)),
                   state_spec, state_spec],
        out_shape=[seq] * 3 + [stk] * 3 + [seq, jax.ShapeDtypeStruct((b, n, CONV_CH), BF16),
                   jax.ShapeDtypeStruct((b, n // CHUNK, N_PAIRS, LANES, LANES), BF16),
                   state, state],
        scratch_shapes=[pltpu.VMEM((tm, RET_WIDTH), BF16),
                        pltpu.VMEM((kfin_rows, RET_WIDTH), BF16),
                        pltpu.VMEM((tm, RET_WIDTH), BF16),
                        pltpu.VMEM((N_PAIRS, LANES, LANES), F32),
                        pltpu.VMEM((N_PAIRS, LANES, LANES), F32)],
        compiler_params=_params(),
        name="inproj",
    )(x, mod, cos_t, sin_t, w_in, tabs["rf"], tabs["rb"], tabs["kdf"], tabs["kdb"], tabs["cdb"], kfin, sb0)


def _mixer_kernel(q_ref, qf_ref, qb_ref, kk_ref, kfs_ref, vv_ref, sg_ref, u_ref, up_ref, un_ref, sbc_ref,
                  sf0_ref, x_ref, mod_ref, mask_ref, cdf_ref, gn_ref, cw_ref, cv_ref, ones_ref, wout_ref, ln_ref,
                  o_ref, sffin_ref,
                  sf_scr, sfb_scr, pr_scr, o_scr, uf_scr, y_scr, mix_scr, *, tm):
    t = pl.program_id(1)
    nt = pl.num_programs(1)
    cpt = tm // CHUNK
    n_iter = cpt * N_PAIRS // 2

    @pl.when(t == 0)
    def _():
        sf_scr[...] = sf0_ref[0]
        sfb_scr[...] = sf0_ref[0].astype(BF16)

    contract_rows = (((0,), (0,)), ((), ()))
    contract_lanes = (((1,), (1,)), ((), ()))

    def unit_of(it, e):
        c = it % cpt
        return (2 * (it // cpt) + e, pl.ds(pl.multiple_of(c * CHUNK, CHUNK), CHUNK),
                pl.ds(pl.multiple_of(2 * c * CHUNK, 2 * CHUNK), 2 * CHUNK))

    def probs_into(slot, it):
        for e in range(2):
            j, rows, srows = unit_of(it, e)
            sc = lax.dot_general(q_ref[0, j, rows, :], kk_ref[0, j, srows, :], contract_lanes,
                                 preferred_element_type=F32)
            pr_scr[slot, e] = (sc * mask_ref[j]).astype(BF16)

    def conv_unit(idx):
        slab = idx % N_SLABS
        base = (idx // N_SLABS) * CHUNK
        first = CONV_HALO - CONV_WIDTH // 2
        bias = jnp.broadcast_to(cw_ref[slab, CONV_WIDTH:CONV_WIDTH + 1, :], (HALF, LANES))
        accs = [[bias, None] for _ in range(2)]
        for tap in range(CONV_WIDTH):
            w = cw_ref[slab, tap:tap + 1, :]
            for par in range(2):
                term = w * uf_scr[slab, pl.ds(base + par + first + tap, HALF, stride=2), :]
                prev = accs[par][tap % 2]
                accs[par][tap % 2] = term if prev is None else prev + term
        for par in range(2):
            y_scr[slab, pl.ds(base + par, HALF, stride=2), :] = accs[par][0] + accs[par][1]

    probs_into(0, 0)

    for j in range(N_SLABS):
        lanes = slice(j * LANES, (j + 1) * LANES)
        uf_scr[j, 0:CONV_HALO, :] = jnp.where(t > 0, up_ref[0, :, lanes].astype(F32), 0.0)
        uf_scr[j, CONV_HALO:CONV_HALO + tm, :] = u_ref[0, :, lanes].astype(F32)
        uf_scr[j, CONV_HALO + tm:, :] = jnp.where(t < nt - 1, un_ref[0, :, lanes].astype(F32), 0.0)

    def step(it, carry):
        c = it % cpt
        units = [unit_of(it, e) for e in range(2)]
        probs = [pr_scr[it % 2, e] for e in range(2)]
        states = [sf_scr[j] for j, _, _ in units]
        states_b = [sfb_scr[j] for j, _, _ in units]
        new_states = []
        for e, (j, rows, srows) in enumerate(units):
            vv = vv_ref[0, j, srows, :]
            lhs = jnp.concatenate([probs[e], qf_ref[0, j, rows, :], qb_ref[0, j, rows, :]], axis=1)
            rhs = jnp.concatenate([vv, states_b[e], sbc_ref[0, c, j]], axis=0)
            o_scr[j, rows, :] = jnp.dot(lhs, rhs, preferred_element_type=F32)
            kv = lax.dot_general(kfs_ref[0, j, srows, :], vv, contract_rows, preferred_element_type=F32)
            new_states.append(states[e] * cdf_ref[j] + kv)
        probs_into((it + 1) % 2, jnp.minimum(it + 1, n_iter - 1))
        for e, (j, _, _) in enumerate(units):
            sf_scr[j] = new_states[e]
            sfb_scr[j] = new_states[e].astype(BF16)
        conv_unit(2 * it)
        conv_unit(2 * it + 1)
        return carry

    lax.fori_loop(0, n_iter, step, 0)

    gate = mod_ref[0, 2:3, :]
    group = min(tm, ROW_GROUP)
    for g in range(tm // group):
        rows = slice(g * group, (g + 1) * group)
        for i in range(2):
            lanes = slice(i * 2 * LANES, (i + 1) * 2 * LANES)
            o = jnp.concatenate([o_scr[2 * i, rows, :], o_scr[2 * i + 1, rows, :]], axis=1)
            d = o - jnp.dot(o.astype(BF16), ones_ref[...], preferred_element_type=F32)
            var = jnp.dot((d * d).astype(BF16), ones_ref[...], preferred_element_type=F32)
            on = d * lax.rsqrt(var + LN_EPS) * gn_ref[0:1, lanes] + gn_ref[1:2, lanes]
            sg = jnp.concatenate([sg_ref[0, 2 * i, rows, :], sg_ref[0, 2 * i + 1, rows, :]], axis=1).astype(F32)
            mix_scr[rows, lanes] = (sg * on).astype(BF16)
        y = _layer_norm(jnp.concatenate([y_scr[j, rows, :] for j in range(N_SLABS)], axis=1),
                        cv_ref[0:1, :], cv_ref[1:2, :])
        mix_scr[rows, RET_WIDTH:] = (y * jax.nn.sigmoid(y)).astype(BF16)
        mix = jnp.dot(mix_scr[rows, :], wout_ref[...], preferred_element_type=F32)
        o_ref[0, rows, :] = _layer_norm(ALPHA * x_ref[0, rows, :] + gate * mix, ln_ref[0:1, :], ln_ref[1:2, :])

    @pl.when(t == nt - 1)
    def _():
        sffin_ref[0] = sf_scr[...]


def _group_ones():
    g = jnp.arange(2 * LANES) // HEAD_DIM
    return jnp.where(g[:, None] == g[None, :], 1.0 / HEAD_DIM, 0.0).astype(BF16)


def _mixer(layer, q, qf, qb, kk, kfs, vv, sg, u, sbc, sf0, x, mod, tabs, gn, cw, cv, w_out, ln1, *, tm):
    b, n, _ = x.shape
    nt = n // tm
    cpt = tm // CHUNK
    hpt = tm // CONV_HALO
    last_halo = n // CONV_HALO - 1
    tile = lambda bb, t: (bb, t, 0)
    seq_spec = pl.BlockSpec((1, N_PAIRS, tm, LANES), lambda bb, t: (bb, 0, t, 0))
    stk_spec = pl.BlockSpec((1, N_PAIRS, 2 * tm, LANES), lambda bb, t: (bb, 0, t, 0))
    state_spec = pl.BlockSpec((1, N_PAIRS, LANES, LANES), lambda bb, t: (bb, 0, 0, 0))
    ones = _group_ones()
    return pl.pallas_call(
        functools.partial(_mixer_kernel, tm=tm),
        grid=(b, nt),
        in_specs=[seq_spec] * 3 + [stk_spec] * 3 + [seq_spec,
                  pl.BlockSpec((1, tm, CONV_CH), tile),
                  pl.BlockSpec((1, CONV_HALO, CONV_CH),
                               lambda bb, t: (bb, jnp.maximum(t * hpt - 1, 0), 0)),
                  pl.BlockSpec((1, CONV_HALO, CONV_CH),
                               lambda bb, t: (bb, jnp.minimum((t + 1) * hpt, last_halo), 0)),
                  pl.BlockSpec((1, cpt, N_PAIRS, LANES, LANES), lambda bb, t: (bb, t, 0, 0, 0)),
                  state_spec,
                  pl.BlockSpec((1, tm, D_MODEL), tile),
                  pl.BlockSpec((None, 1, 8, D_MODEL), lambda bb, t: (layer, bb if mod.shape[1] > 1 else 0, 0, 0)),
                  _layer_resident(layer, (N_PAIRS, CHUNK, 2 * CHUNK)),
                  _layer_resident(layer, (N_PAIRS, 1, LANES)),
                  _layer_resident(layer, (2, RET_WIDTH)),
                  _layer_resident(layer, (N_SLABS, 32, LANES)),
                  _layer_resident(layer, (2, CONV_CH)),
                  _resident(ones.shape),
                  _layer_resident(layer, (D_MODEL, D_MODEL)),
                  _layer_resident(layer, (2, D_MODEL))],
        out_specs=[pl.BlockSpec((1, tm, D_MODEL), tile), state_spec],
        out_shape=[jax.ShapeDtypeStruct((b, n, D_MODEL), F32),
                   jax.ShapeDtypeStruct((b, N_PAIRS, LANES, LANES), F32)],
        scratch_shapes=[pltpu.VMEM((N_PAIRS, LANES, LANES), F32),
                        pltpu.VMEM((N_PAIRS, LANES, LANES), BF16),
                        pltpu.VMEM((2, 2, CHUNK, 2 * CHUNK), BF16),
                        pltpu.VMEM((N_PAIRS, tm, LANES), F32),
                        pltpu.VMEM((N_SLABS, tm + 2 * CONV_HALO, LANES), F32),
                        pltpu.VMEM((N_SLABS, tm, LANES), F32),
                        pltpu.VMEM((tm, 2 * RET_WIDTH), BF16)],
        compiler_params=_params(),
        name="mixer",
    )(q, qf, qb, kk, kfs, vv, sg, u, u, u, sbc, sf0, x, mod, tabs["mask"], tabs["cdf"], gn, cw, cv, ones, w_out,
      ln1)


def _mlp_kernel(x_ref, mod_ref, w1_ref, w2_ref, ln_ref, o_ref, *, sub):
    for i in range(x_ref.shape[1] // sub):
        rows = slice(i * sub, (i + 1) * sub)
        x = x_ref[0, rows, :]
        h = (x * (1.0 + mod_ref[0, 4:5, :]) + mod_ref[0, 3:4, :]).astype(BF16)
        y = jnp.zeros(x.shape, F32)
        for j in range(D_FF // FF_BLOCK):
            cols = slice(j * FF_BLOCK, (j + 1) * FF_BLOCK)
            a = jnp.maximum(jnp.dot(h, w1_ref[:, cols], preferred_element_type=F32), 0.0)
            y = y + jnp.dot((a * a).astype(BF16), w2_ref[cols, :], preferred_element_type=F32)
        o_ref[0, rows, :] = _layer_norm(ALPHA * x + mod_ref[0, 5:6, :] * y, ln_ref[0:1, :], ln_ref[1:2, :])


def _mlp(layer, x, mod, w1, w2, ln2, *, tm):
    b, n, _ = x.shape
    tile = lambda bb, t: (bb, t, 0)
    return pl.pallas_call(
        functools.partial(_mlp_kernel, sub=min(tm, MLP_ROWS)),
        grid=(b, n // tm),
        in_specs=[pl.BlockSpec((1, tm, D_MODEL), tile),
                  pl.BlockSpec((None, 1, 8, D_MODEL), lambda bb, t: (layer, bb if mod.shape[1] > 1 else 0, 0, 0)),
                  _layer_resident(layer, (D_MODEL, D_FF)),
                  _layer_resident(layer, (D_FF, D_MODEL)),
                  _layer_resident(layer, (2, D_MODEL))],
        out_specs=pl.BlockSpec((1, tm, D_MODEL), tile),
        out_shape=jax.ShapeDtypeStruct((b, n, D_MODEL), F32),
        compiler_params=_params(),
        name="mlp",
    )(x, mod, w1, w2, ln2)


def _rope_tables(n):
    pos = jnp.arange(n, dtype=jnp.int32)
    row = (pos // GRID_W).astype(F32)[:, None]
    col = (pos % GRID_W).astype(F32)[:, None]
    n_freq = HEAD_DIM // 4
    lane = jnp.arange(LANES, dtype=jnp.int32)
    pair = (lane % HEAD_DIM) // 2
    inv = ROPE_BASE ** (-(pair % n_freq).astype(F32) / n_freq)
    ang = jnp.where(pair < n_freq, row, col) * inv
    sign = jnp.where(lane % 2 == 0, -1.0, 1.0)
    return jnp.cos(ang), jnp.sin(ang) * sign


def _per_lane(t):
    return jnp.repeat(jnp.swapaxes(t, 1, 2), HEAD_DIM, axis=2)


def _decay_tables(lg_f, lg_b, tm):
    idx = jnp.arange(CHUNK, dtype=F32)
    lf = lg_f.astype(F32)[:, :, None]
    lb = lg_b.astype(F32)[:, :, None]
    diff = idx[:, None] - idx[None, :]
    fwd = jnp.where(diff >= 0, jnp.exp(lf[..., None] * jnp.maximum(diff, 0.0)), 0.0)
    bwd = jnp.where(diff <= 0, jnp.exp(lb[..., None] * jnp.maximum(-diff, 0.0)), 0.0)
    mask = (fwd + bwd).reshape(DEPTH, N_PAIRS, 2, CHUNK, CHUNK).transpose(0, 1, 3, 2, 4)
    reps = tm // CHUNK
    tiled = lambda t: jnp.tile(_per_lane(t), (1, reps, 1))
    return {
        "mask": mask.reshape(DEPTH, N_PAIRS, CHUNK, 2 * CHUNK),
        "rf": tiled(jnp.exp(lf * (idx + 1.0))),
        "rb": tiled(jnp.exp(lb * (CHUNK - idx))),
        "kdf": tiled(jnp.exp(lf * (CHUNK - 1.0 - idx))),
        "kdb": tiled(jnp.exp(lb * idx)),
        "cdf": _per_lane(jnp.exp(lf * CHUNK)).reshape(DEPTH, N_PAIRS, 1, LANES),
        "cdb": _per_lane(jnp.exp(lb * CHUNK)),
    }


def _final_weights(lg_f, n):
    pos = jnp.arange(n, dtype=F32)
    return _per_lane(jnp.exp(lg_f.astype(F32)[:, :, None] * (n - 1.0 - pos)))


def kernel(x, c, ctx, c_ctx, w_ada, b_ada, w_in, ret_log_rate_fwd, ret_log_rate_bwd, ret_gn_w, ret_gn_b,
           conv_w, conv_b, conv_ln_w, conv_ln_b, w_out, ln1_w, ln1_b, w_ff1, w_ff2, ln2_w, ln2_b):
    b, n, _ = x.shape
    n_ctx = ctx.shape[1]
    tm_lat, tm_ctx, tm_mlp = 512, n_ctx, 1024
    w_in, w_out, w_ff1, w_ff2 = (w.astype(BF16) for w in (w_in, w_out, w_ff1, w_ff2))

    cond = jnp.zeros((8, D_MODEL), F32).at[:b].set(c).at[b].set(c_ctx)
    ada = _adaln(cond, w_ada, b_ada).reshape(DEPTH, 8, 6, D_MODEL)
    mods = jnp.pad(ada, ((0, 0), (0, 0), (0, 2), (0, 0)))
    mod_lat, mod_ctx = mods[:, :b], mods[:, b:b + 1]

    cos_lat, sin_lat = _rope_tables(n)
    cos_ctx = jnp.ones((n_ctx, LANES), F32)
    sin_ctx = jnp.zeros((n_ctx, LANES), F32)
    zero_state = jnp.zeros((b, N_PAIRS, LANES, LANES), F32)

    lg_f = -jnp.exp(ret_log_rate_fwd)
    lg_b = -jnp.exp(ret_log_rate_bwd)
    tabs = _decay_tables(lg_f, lg_b, tm_lat)
    kfin_ctx = _final_weights(lg_f, n_ctx)
    gn = jnp.stack([ret_gn_w, ret_gn_b], axis=1)
    cw = jnp.concatenate([conv_w, conv_b[:, None]], axis=1).reshape(DEPTH, 32, N_SLABS, LANES).transpose(0, 2, 1, 3)
    cv = jnp.stack([conv_ln_w, conv_ln_b], axis=1)
    ln1 = jnp.stack([ln1_w, ln1_b], axis=1)
    ln2 = jnp.stack([ln2_w, ln2_b], axis=1)

    for l in range(DEPTH):
        last = l == DEPTH - 1
        k_scale_ctx = 1.0 if last else HEAD_DIM ** -0.5
        (qc, qfc, qbc, kkc, kfsc, vvc, sgc, uc, sbc_c, sb_fin, sf_fin) = _inproj(
            l, ctx, mod_ctx, cos_ctx, sin_ctx, w_in, tabs, kfin_ctx, zero_state,
            tm=tm_ctx, k_scale=k_scale_ctx)

        (q, qf, qb, kk, kfs, vv, sg, u, sbc, _, _) = _inproj(
            l, x, mod_lat, cos_lat, sin_lat, w_in, tabs, None, sb_fin,
            tm=tm_lat, k_scale=HEAD_DIM ** -0.5)
        x, _ = _mixer(l, q, qf, qb, kk, kfs, vv, sg, u, sbc, sf_fin, x, mod_lat, tabs, gn, cw, cv, w_out, ln1,
                      tm=tm_lat)
        x = _mlp(l, x, mod_lat, w_ff1, w_ff2, ln2, tm=tm_mlp)

        if not last:
            ctx, _ = _mixer(l, qc, qfc, qbc, kkc, kfsc, vvc, sgc, uc, sbc_c, zero_state, ctx, mod_ctx, tabs, gn,
                            cw, cv, w_out, ln1, tm=tm_ctx)
            ctx = _mlp(l, ctx, mod_ctx, w_ff1, w_ff2, ln2, tm=tm_ctx)

    return x
```

```python
import functools

import jax
import jax.numpy as jnp
from jax import lax
from jax.experimental import pallas as pl
from jax.experimental.pallas import tpu as pltpu

D_MODEL = 1024
DEPTH = 4
GRID_W = 64
RET_HEADS = 8
HEAD_DIM = 64
RET_WIDTH = RET_HEADS * HEAD_DIM
CONV_CH = 512
CONV_WIDTH = 31
CONV_HALO = 16
IN_WIDTH = 3072
D_FF = 4 * D_MODEL
CHUNK = 128
ROPE_BASE = 10000.0
LN_EPS = 1e-5
ALPHA = (2 * DEPTH) ** 0.25
LANES = 128
N_PAIRS = RET_HEADS // 2
FF_BLOCK = 1024
ROW_GROUP = 512
MLP_ROWS = 256
N_SLABS = CONV_CH // LANES
HALF = CHUNK // 2
VMEM_LIMIT = 56 * 1024 * 1024

F32 = jnp.float32
BF16 = jnp.bfloat16


def _resident(shape):
    zeros = (0,) * len(shape)
    return pl.BlockSpec(shape, lambda *_: zeros, pipeline_mode=pl.Buffered(1))


def _layer_resident(layer, shape):
    zeros = (0,) * len(shape)
    return pl.BlockSpec((None,) + tuple(shape), lambda *_: (layer,) + zeros, pipeline_mode=pl.Buffered(1))


def _params():
    return pltpu.CompilerParams(dimension_semantics=("arbitrary", "arbitrary"),
                                vmem_limit_bytes=VMEM_LIMIT)


def _layer_norm(x, w, b):
    mu = jnp.mean(x, axis=-1, keepdims=True)
    d = x - mu
    var = jnp.mean(d * d, axis=-1, keepdims=True)
    return d * lax.rsqrt(var + LN_EPS) * w + b


def _pair_masks(rows):
    lane = lax.broadcasted_iota(jnp.int32, (rows, LANES), 1)
    return lane < HEAD_DIM, lane >= HEAD_DIM


def _block_diag_mask():
    r = lax.broadcasted_iota(jnp.int32, (LANES, LANES), 0)
    c = lax.broadcasted_iota(jnp.int32, (LANES, LANES), 1)
    return ((r < HEAD_DIM) == (c < HEAD_DIM)).astype(F32)


def _split_bf16(t):
    hi = t.astype(BF16)
    return hi, (t - hi.astype(F32)).astype(BF16)


def _adaln_kernel(cond_ref, w_ref, b_ref, o_ref):
    cond = cond_ref[...]
    a_hi, a_lo = _split_bf16(cond * jax.nn.sigmoid(cond))
    w_hi, w_lo = _split_bf16(w_ref[0])
    acc = jnp.dot(a_hi, w_lo, preferred_element_type=F32) + jnp.dot(a_lo, w_hi, preferred_element_type=F32)
    o_ref[0] = acc + jnp.dot(a_hi, w_hi, preferred_element_type=F32) + b_ref[0]


def _adaln(cond, w_ada, b_ada):
    width = w_ada.shape[-1]
    block = 1536
    return pl.pallas_call(
        _adaln_kernel,
        grid=(DEPTH, width // block),
        in_specs=[pl.BlockSpec((8, D_MODEL), lambda l, j: (0, 0)),
                  pl.BlockSpec((1, D_MODEL, block), lambda l, j: (l, 0, j)),
                  pl.BlockSpec((1, 1, block), lambda l, j: (l, 0, j))],
        out_specs=pl.BlockSpec((1, 8, block), lambda l, j: (l, 0, j)),
        out_shape=jax.ShapeDtypeStruct((DEPTH, 8, width), F32),
        compiler_params=_params(),
        name="adaln",
    )(cond, w_ada, b_ada.reshape(DEPTH, 1, width))


def _inproj_kernel(x_ref, mod_ref, cos_ref, sin_ref, w_ref, rf_ref, rb_ref, kdf_ref, kdb_ref, cdb_ref,
                   kfin_ref, sb0_ref,
                   q_ref, qf_ref, qb_ref, kk_ref, kfs_ref, vv_ref, sg_ref, u_ref, sbc_ref, sbfin_ref, sffin_ref,
                   kb_scr, kfin_scr, v_scr, sb_scr, sf_scr, *, tm, k_scale, want_final_fwd):
    t = pl.program_id(1)
    nt = pl.num_programs(1)
    cpt = tm // CHUNK

    @pl.when(t == 0)
    def _():
        sb_scr[...] = sb0_ref[0]
        sf_scr[...] = jnp.zeros_like(sf_scr)

    shift = mod_ref[0, 0:1, :]
    scale = mod_ref[0, 1:2, :]
    h = (x_ref[0] * (1.0 + scale) + shift).astype(BF16)

    def proj(col):
        return jnp.dot(h, w_ref[:, col:col + RET_WIDTH], preferred_element_type=F32)

    cos = cos_ref[...]
    sin = sin_ref[...]
    even = (lax.broadcasted_iota(jnp.int32, (tm, LANES), 1) % 2) == 0

    def rope(p):
        swapped = jnp.where(even, pltpu.roll(p, LANES - 1, 1), pltpu.roll(p, 1, 1))
        return p * cos + swapped * sin

    def put_stacked(ref, j, t):
        left, right = _pair_masks(CHUNK)
        for c in range(cpt):
            tc = t[c * CHUNK:(c + 1) * CHUNK]
            ref[0, j, 2 * c * CHUNK:(2 * c + 1) * CHUNK, :] = jnp.where(left, tc, 0.0).astype(BF16)
            ref[0, j, (2 * c + 1) * CHUNK:(2 * c + 2) * CHUNK, :] = jnp.where(right, tc, 0.0).astype(BF16)

    pq = proj(0)
    pk = proj(RET_WIDTH)
    for j in range(N_PAIRS):
        lanes = slice(j * LANES, (j + 1) * LANES)
        qq = rope(pq[:, lanes])
        q_ref[0, j] = qq.astype(BF16)
        qf_ref[0, j] = (qq * rf_ref[:, lanes]).astype(BF16)
        qb_ref[0, j] = (qq * rb_ref[:, lanes]).astype(BF16)
        kk = rope(pk[:, lanes]) * k_scale
        put_stacked(kk_ref, j, kk)
        put_stacked(kfs_ref, j, kk * kdf_ref[:, lanes])
        kb_scr[:, lanes] = (kk * kdb_ref[:, lanes]).astype(BF16)
        if want_final_fwd:
            kfin_scr[:, lanes] = (kk * kfin_ref[:, lanes]).astype(BF16)

    pv = proj(2 * RET_WIDTH)
    v_scr[...] = pv.astype(BF16)
    g = proj(3 * RET_WIDTH)
    sg = (g * jax.nn.sigmoid(g)).astype(BF16)
    for j in range(N_PAIRS):
        lanes = slice(j * LANES, (j + 1) * LANES)
        put_stacked(vv_ref, j, pv[:, lanes])
        sg_ref[0, j] = sg[:, lanes]
    a = proj(4 * RET_WIDTH)
    gate = proj(5 * RET_WIDTH)
    u_ref[0] = (a * jax.nn.sigmoid(gate)).astype(BF16)

    bd = _block_diag_mask()
    contract_rows = (((0,), (0,)), ((), ()))
    for c in reversed(range(cpt)):
        rows = slice(c * CHUNK, (c + 1) * CHUNK)
        for j in range(N_PAIRS):
            lanes = slice(j * LANES, (j + 1) * LANES)
            vp = v_scr[rows, lanes]
            s = sb_scr[j]
            sbc_ref[0, c, j] = s.astype(BF16)
            kv = lax.dot_general(kb_scr[rows, lanes], vp, contract_rows, preferred_element_type=F32)
            sb_scr[j] = s * cdb_ref[:, lanes] + kv * bd
            if want_final_fwd:
                kvf = lax.dot_general(kfin_scr[rows, lanes], vp, contract_rows,
                                      preferred_element_type=F32)
                sf_scr[j] = sf_scr[j] + kvf * bd

    @pl.when(t == nt - 1)
    def _():
        sbfin_ref[0] = sb_scr[...]
        sffin_ref[0] = sf_scr[...]


def _inproj(layer, x, mod, cos_t, sin_t, w_in, tabs, kfin, sb0, *, tm, k_scale):
    b, n, _ = x.shape
    nt = n // tm
    cpt = tm // CHUNK
    rev = lambda bb, t: (bb, nt - 1 - t, 0)
    rev_tab = lambda bb, t: (nt - 1 - t, 0)
    seq = jax.ShapeDtypeStruct((b, N_PAIRS, n, LANES), BF16)
    state = jax.ShapeDtypeStruct((b, N_PAIRS, LANES, LANES), F32)
    seq_spec = pl.BlockSpec((1, N_PAIRS, tm, LANES), lambda bb, t: (bb, 0, nt - 1 - t, 0))
    stk = jax.ShapeDtypeStruct((b, N_PAIRS, 2 * n, LANES), BF16)
    stk_spec = pl.BlockSpec((1, N_PAIRS, 2 * tm, LANES), lambda bb, t: (bb, 0, nt - 1 - t, 0))
    state_spec = pl.BlockSpec((1, N_PAIRS, LANES, LANES), lambda bb, t: (bb, 0, 0, 0))
    mod_spec = pl.BlockSpec((None, 1, 8, D_MODEL), lambda bb, t: (layer, bb if mod.shape[1] > 1 else 0, 0, 0))
    want_final_fwd = kfin is not None
    if want_final_fwd:
        kfin_spec = pl.BlockSpec((None, tm, RET_WIDTH), lambda bb, t: (layer, nt - 1 - t, 0))
        kfin_rows = tm
    else:
        kfin = jnp.zeros((8, RET_WIDTH), F32)
        kfin_spec = _resident((8, RET_WIDTH))
        kfin_rows = 16
    return pl.pallas_call(
        functools.partial(_inproj_kernel, tm=tm, k_scale=k_scale, want_final_fwd=want_final_fwd),
        grid=(b, nt),
        in_specs=[pl.BlockSpec((1, tm, D_MODEL), rev),
                  mod_spec,
                  pl.BlockSpec((tm, LANES), rev_tab),
                  pl.BlockSpec((tm, LANES), rev_tab),
                  _layer_resident(layer, (D_MODEL, IN_WIDTH)),
                  _layer_resident(layer, (tm, RET_WIDTH)),
                  _layer_resident(layer, (tm, RET_WIDTH)),
                  _layer_resident(layer, (tm, RET_WIDTH)),
                  _layer_resident(layer, (tm, RET_WIDTH)),
                  _layer_resident(layer, (1, RET_WIDTH)),
                  kfin_spec,
                  state_spec],
        out_specs=[seq_spec] * 3 + [stk_spec] * 3 + [seq_spec, pl.BlockSpec((1, tm, CONV_CH), rev),
                   pl.BlockSpec((1, cpt, N_PAIRS, LANES, LANES),
                                lambda bb, t: (bb, nt - 1 - t, 0, 0, 0)),
                   state_spec, state_spec],
        out_shape=[seq] * 3 + [stk] * 3 + [seq, jax.ShapeDtypeStruct((b, n, CONV_CH), BF16),
                   jax.ShapeDtypeStruct((b, n // CHUNK, N_PAIRS, LANES, LANES), BF16),
                   state, state],
        scratch_shapes=[pltpu.VMEM((tm, RET_WIDTH), BF16),
                        pltpu.VMEM((kfin_rows, RET_WIDTH), BF16),
                        pltpu.VMEM((tm, RET_WIDTH), BF16),
                        pltpu.VMEM((N_PAIRS, LANES, LANES), F32),
                        pltpu.VMEM((N_PAIRS, LANES, LANES), F32)],
        compiler_params=_params(),
        name="inproj",
    )(x, mod, cos_t, sin_t, w_in, tabs["rf"], tabs["rb"], tabs["kdf"], tabs["kdb"], tabs["cdb"], kfin, sb0)


def _mixer_kernel(q_ref, qf_ref, qb_ref, kk_ref, kfs_ref, vv_ref, sg_ref, u_ref, up_ref, un_ref, sbc_ref,
                  sf0_ref, x_ref, mod_ref, mask_ref, cdf_ref, gn_ref, cw_ref, cv_ref, ones_ref, wout_ref, ln_ref,
                  o_ref, sffin_ref,
                  sf_scr, sfb_scr, pr_scr, o_scr, uf_scr, y_scr, mix_scr, *, tm):
    t = pl.program_id(1)
    nt = pl.num_programs(1)
    cpt = tm // CHUNK
    n_iter = cpt * N_PAIRS // 2

    @pl.when(t == 0)
    def _():
        sf_scr[...] = sf0_ref[0]
        sfb_scr[...] = sf0_ref[0].astype(BF16)

    contract_rows = (((0,), (0,)), ((), ()))
    contract_lanes = (((1,), (1,)), ((), ()))

    def unit_of(it, e):
        c = it % cpt
        return (2 * (it // cpt) + e, pl.ds(pl.multiple_of(c * CHUNK, CHUNK), CHUNK),
                pl.ds(pl.multiple_of(2 * c * CHUNK, 2 * CHUNK), 2 * CHUNK))

    def probs_into(slot, it):
        for e in range(2):
            j, rows, srows = unit_of(it, e)
            sc = lax.dot_general(q_ref[0, j, rows, :], kk_ref[0, j, srows, :], contract_lanes,
                                 preferred_element_type=F32)
            pr_scr[slot, e] = (sc * mask_ref[j]).astype(BF16)

    def conv_unit(idx):
        slab = idx % N_SLABS
        base = (idx // N_SLABS) * CHUNK
        first = CONV_HALO - CONV_WIDTH // 2
        bias = jnp.broadcast_to(cw_ref[slab, CONV_WIDTH:CONV_WIDTH + 1, :], (HALF, LANES))
        accs = [[bias, None] for _ in range(2)]
        for tap in range(CONV_WIDTH):
            w = cw_ref[slab, tap:tap + 1, :]
            for par in range(2):
                term = w * uf_scr[slab, pl.ds(base + par + first + tap, HALF, stride=2), :]
                prev = accs[par][tap % 2]
                accs[par][tap % 2] = term if prev is None else prev + term
        for par in range(2):
            y_scr[slab, pl.ds(base + par, HALF, stride=2), :] = accs[par][0] + accs[par][1]

    probs_into(0, 0)

    for j in range(N_SLABS):
        lanes = slice(j * LANES, (j + 1) * LANES)
        uf_scr[j, 0:CONV_HALO, :] = jnp.where(t > 0, up_ref[0, :, lanes].astype(F32), 0.0)
        uf_scr[j, CONV_HALO:CONV_HALO + tm, :] = u_ref[0, :, lanes].astype(F32)
        uf_scr[j, CONV_HALO + tm:, :] = jnp.where(t < nt - 1, un_ref[0, :, lanes].astype(F32), 0.0)

    def step(it, carry):
        c = it % cpt
        units = [unit_of(it, e) for e in range(2)]
        probs = [pr_scr[it % 2, e] for e in range(2)]
        states = [sf_scr[j] for j, _, _ in units]
        states_b = [sfb_scr[j] for j, _, _ in units]
        new_states = []
        for e, (j, rows, srows) in enumerate(units):
            vv = vv_ref[0, j, srows, :]
            lhs = jnp.concatenate([probs[e], qf_ref[0, j, rows, :], qb_ref[0, j, rows, :]], axis=1)
            rhs = jnp.concatenate([vv, states_b[e], sbc_ref[0, c, j]], axis=0)
            o_scr[j, rows, :] = jnp.dot(lhs, rhs, preferred_element_type=F32)
            kv = lax.dot_general(kfs_ref[0, j, srows, :], vv, contract_rows, preferred_element_type=F32)
            new_states.append(states[e] * cdf_ref[j] + kv)
        probs_into((it + 1) % 2, jnp.minimum(it + 1, n_iter - 1))
        for e, (j, _, _) in enumerate(units):
            sf_scr[j] = new_states[e]
            sfb_scr[j] = new_states[e].astype(BF16)
        conv_unit(2 * it)
        conv_unit(2 * it + 1)
        return carry

    lax.fori_loop(0, n_iter, step, 0)

    gate = mod_ref[0, 2:3, :]
    group = min(tm, ROW_GROUP)
    for g in range(tm // group):
        rows = slice(g * group, (g + 1) * group)
        for i in range(2):
            lanes = slice(i * 2 * LANES, (i + 1) * 2 * LANES)
            o = jnp.concatenate([o_scr[2 * i, rows, :], o_scr[2 * i + 1, rows, :]], axis=1)
            d = o - jnp.dot(o.astype(BF16), ones_ref[...], preferred_element_type=F32)
            var = jnp.dot((d * d).astype(BF16), ones_ref[...], preferred_element_type=F32)
            on = d * lax.rsqrt(var + LN_EPS) * gn_ref[0:1, lanes] + gn_ref[1:2, lanes]
            sg = jnp.concatenate([sg_ref[0, 2 * i, rows, :], sg_ref[0, 2 * i + 1, rows, :]], axis=1).astype(F32)
            mix_scr[rows, lanes] = (sg * on).astype(BF16)
        y = _layer_norm(jnp.concatenate([y_scr[j, rows, :] for j in range(N_SLABS)], axis=1),
                        cv_ref[0:1, :], cv_ref[1:2, :])
        mix_scr[rows, RET_WIDTH:] = (y * jax.nn.sigmoid(y)).astype(BF16)
        mix = jnp.dot(mix_scr[rows, :], wout_ref[...], preferred_element_type=F32)
        o_ref[0, rows, :] = _layer_norm(ALPHA * x_ref[0, rows, :] + gate * mix, ln_ref[0:1, :], ln_ref[1:2, :])

    @pl.when(t == nt - 1)
    def _():
        sffin_ref[0] = sf_scr[...]


def _group_ones():
    g = jnp.arange(2 * LANES) // HEAD_DIM
    return jnp.where(g[:, None] == g[None, :], 1.0 / HEAD_DIM, 0.0).astype(BF16)


def _mixer(layer, q, qf, qb, kk, kfs, vv, sg, u, sbc, sf0, x, mod, tabs, gn, cw, cv, w_out, ln1, *, tm):
    b, n, _ = x.shape
    nt = n // tm
    cpt = tm // CHUNK
    hpt = tm // CONV_HALO
    last_halo = n // CONV_HALO - 1
    tile = lambda bb, t: (bb, t, 0)
    seq_spec = pl.BlockSpec((1, N_PAIRS, tm, LANES), lambda bb, t: (bb, 0, t, 0))
    stk_spec = pl.BlockSpec((1, N_PAIRS, 2 * tm, LANES), lambda bb, t: (bb, 0, t, 0))
    state_spec = pl.BlockSpec((1, N_PAIRS, LANES, LANES), lambda bb, t: (bb, 0, 0, 0))
    ones = _group_ones()
    return pl.pallas_call(
        functools.partial(_mixer_kernel, tm=tm),
        grid=(b, nt),
        in_specs=[seq_spec] * 3 + [stk_spec] * 3 + [seq_spec,
                  pl.BlockSpec((1, tm, CONV_CH), tile),
                  pl.BlockSpec((1, CONV_HALO, CONV_CH),
                               lambda bb, t: (bb, jnp.maximum(t * hpt - 1, 0), 0)),
                  pl.BlockSpec((1, CONV_HALO, CONV_CH),
                               lambda bb, t: (bb, jnp.minimum((t + 1) * hpt, last_halo), 0)),
                  pl.BlockSpec((1, cpt, N_PAIRS, LANES, LANES), lambda bb, t: (bb, t, 0, 0, 0)),
                  state_spec,
                  pl.BlockSpec((1, tm, D_MODEL), tile),
                  pl.BlockSpec((None, 1, 8, D_MODEL), lambda bb, t: (layer, bb if mod.shape[1] > 1 else 0, 0, 0)),
                  _layer_resident(layer, (N_PAIRS, CHUNK, 2 * CHUNK)),
                  _layer_resident(layer, (N_PAIRS, 1, LANES)),
                  _layer_resident(layer, (2, RET_WIDTH)),
                  _layer_resident(layer, (N_SLABS, 32, LANES)),
                  _layer_resident(layer, (2, CONV_CH)),
                  _resident(ones.shape),
                  _layer_resident(layer, (D_MODEL, D_MODEL)),
                  _layer_resident(layer, (2, D_MODEL))],
        out_specs=[pl.BlockSpec((1, tm, D_MODEL), tile), state_spec],
        out_shape=[jax.ShapeDtypeStruct((b, n, D_MODEL), F32),
                   jax.ShapeDtypeStruct((b, N_PAIRS, LANES, LANES), F32)],
        scratch_shapes=[pltpu.VMEM((N_PAIRS, LANES, LANES), F32),
                        pltpu.VMEM((N_PAIRS, LANES, LANES), BF16),
                        pltpu.VMEM((2, 2, CHUNK, 2 * CHUNK), BF16),
                        pltpu.VMEM((N_PAIRS, tm, LANES), F32),
                        pltpu.VMEM((N_SLABS, tm + 2 * CONV_HALO, LANES), F32),
                        pltpu.VMEM((N_SLABS, tm, LANES), F32),
                        pltpu.VMEM((tm, 2 * RET_WIDTH), BF16)],
        compiler_params=_params(),
        name="mixer",
    )(q, qf, qb, kk, kfs, vv, sg, u, u, u, sbc, sf0, x, mod, tabs["mask"], tabs["cdf"], gn, cw, cv, ones, w_out,
      ln1)


def _mlp_kernel(x_ref, mod_ref, w1_ref, w2_ref, ln_ref, o_ref, *, sub):
    for i in range(x_ref.shape[1] // sub):
        rows = slice(i * sub, (i + 1) * sub)
        x = x_ref[0, rows, :]
        h = (x * (1.0 + mod_ref[0, 4:5, :]) + mod_ref[0, 3:4, :]).astype(BF16)
        y = jnp.zeros(x.shape, F32)
        for j in range(D_FF // FF_BLOCK):
            cols = slice(j * FF_BLOCK, (j + 1) * FF_BLOCK)
            a = jnp.maximum(jnp.dot(h, w1_ref[:, cols], preferred_element_type=F32), 0.0)
            y = y + jnp.dot((a * a).astype(BF16), w2_ref[cols, :], preferred_element_type=F32)
        o_ref[0, rows, :] = _layer_norm(ALPHA * x + mod_ref[0, 5:6, :] * y, ln_ref[0:1, :], ln_ref[1:2, :])


def _mlp(layer, x, mod, w1, w2, ln2, *, tm):
    b, n, _ = x.shape
    tile = lambda bb, t: (bb, t, 0)
    return pl.pallas_call(
        functools.partial(_mlp_kernel, sub=min(tm, MLP_ROWS)),
        grid=(b, n // tm),
        in_specs=[pl.BlockSpec((1, tm, D_MODEL), tile),
                  pl.BlockSpec((None, 1, 8, D_MODEL), lambda bb, t: (layer, bb if mod.shape[1] > 1 else 0, 0, 0)),
                  _layer_resident(layer, (D_MODEL, D_FF)),
                  _layer_resident(layer, (D_FF, D_MODEL)),
                  _layer_resident(layer, (2, D_MODEL))],
        out_specs=pl.BlockSpec((1, tm, D_MODEL), tile),
        out_shape=jax.ShapeDtypeStruct((b, n, D_MODEL), F32),
        compiler_params=_params(),
        name="mlp",
    )(x, mod, w1, w2, ln2)


def _rope_tables(n):
    rows = n // GRID_W
    n_freq = HEAD_DIM // 4
    inv = ROPE_BASE ** (-jnp.arange(n_freq, dtype=F32) / n_freq)
    row_ang = jnp.arange(rows, dtype=F32)[:, None] * inv
    col_ang = jnp.arange(GRID_W, dtype=F32)[:, None] * inv

    def expand(by_row, by_col):
        t = jnp.concatenate([jnp.broadcast_to(by_row[:, None, :], (rows, GRID_W, n_freq)),
                             jnp.broadcast_to(by_col[None, :, :], (rows, GRID_W, n_freq))], axis=-1)
        return jnp.tile(jnp.repeat(t, 2, axis=-1), (1, 1, LANES // HEAD_DIM)).reshape(n, LANES)

    sign = jnp.tile(jnp.array([-1.0, 1.0], F32), LANES // 2)
    return expand(jnp.cos(row_ang), jnp.cos(col_ang)), expand(jnp.sin(row_ang), jnp.sin(col_ang)) * sign


def _per_lane(t):
    return jnp.repeat(jnp.swapaxes(t, 1, 2), HEAD_DIM, axis=2)


def _decay_tables(lg_f, lg_b, tm):
    idx = jnp.arange(CHUNK, dtype=F32)
    lf = lg_f.astype(F32)[:, :, None]
    lb = lg_b.astype(F32)[:, :, None]
    diff = idx[:, None] - idx[None, :]
    fwd = jnp.where(diff >= 0, jnp.exp(lf[..., None] * jnp.maximum(diff, 0.0)), 0.0)
    bwd = jnp.where(diff <= 0, jnp.exp(lb[..., None] * jnp.maximum(-diff, 0.0)), 0.0)
    mask = (fwd + bwd).reshape(DEPTH, N_PAIRS, 2, CHUNK, CHUNK).transpose(0, 1, 3, 2, 4)
    reps = tm // CHUNK
    tiled = lambda t: jnp.tile(_per_lane(t), (1, reps, 1))
    return {
        "mask": mask.reshape(DEPTH, N_PAIRS, CHUNK, 2 * CHUNK),
        "rf": tiled(jnp.exp(lf * (idx + 1.0))),
        "rb": tiled(jnp.exp(lb * (CHUNK - idx))),
        "kdf": tiled(jnp.exp(lf * (CHUNK - 1.0 - idx))),
        "kdb": tiled(jnp.exp(lb * idx)),
        "cdf": _per_lane(jnp.exp(lf * CHUNK)).reshape(DEPTH, N_PAIRS, 1, LANES),
        "cdb": _per_lane(jnp.exp(lb * CHUNK)),
    }


def _final_weights(lg_f, n):
    pos = jnp.arange(n, dtype=F32)
    return _per_lane(jnp.exp(lg_f.astype(F32)[:, :, None] * (n - 1.0 - pos)))


def kernel(x, c, ctx, c_ctx, w_ada, b_ada, w_in, ret_log_rate_fwd, ret_log_rate_bwd, ret_gn_w, ret_gn_b,
           conv_w, conv_b, conv_ln_w, conv_ln_b, w_out, ln1_w, ln1_b, w_ff1, w_ff2, ln2_w, ln2_b):
    b, n, _ = x.shape
    n_ctx = ctx.shape[1]
    tm_lat, tm_ctx, tm_mlp = 512, n_ctx, 1024
    w_in, w_out, w_ff1, w_ff2 = (w.astype(BF16) for w in (w_in, w_out, w_ff1, w_ff2))

    cond = jnp.zeros((8, D_MODEL), F32).at[:b].set(c).at[b].set(c_ctx)
    ada = _adaln(cond, w_ada, b_ada).reshape(DEPTH, 8, 6, D_MODEL)
    mods = jnp.pad(ada, ((0, 0), (0, 0), (0, 2), (0, 0)))
    mod_lat, mod_ctx = mods[:, :b], mods[:, b:b + 1]

    cos_lat, sin_lat = _rope_tables(n)
    cos_ctx = jnp.ones((n_ctx, LANES), F32)
    sin_ctx = jnp.zeros((n_ctx, LANES), F32)
    zero_state = jnp.zeros((b, N_PAIRS, LANES, LANES), F32)

    lg_f = -jnp.exp(ret_log_rate_fwd)
    lg_b = -jnp.exp(ret_log_rate_bwd)
    tabs = _decay_tables(lg_f, lg_b, tm_lat)
    kfin_ctx = _final_weights(lg_f, n_ctx)
    gn = jnp.stack([ret_gn_w, ret_gn_b], axis=1)
    cw = jnp.concatenate([conv_w, conv_b[:, None]], axis=1).reshape(DEPTH, 32, N_SLABS, LANES).transpose(0, 2, 1, 3)
    cv = jnp.stack([conv_ln_w, conv_ln_b], axis=1)
    ln1 = jnp.stack([ln1_w, ln1_b], axis=1)
    ln2 = jnp.stack([ln2_w, ln2_b], axis=1)

    for l in range(DEPTH):
        last = l == DEPTH - 1
        k_scale_ctx = 1.0 if last else HEAD_DIM ** -0.5
        (qc, qfc, qbc, kkc, kfsc, vvc, sgc, uc, sbc_c, sb_fin, sf_fin) = _inproj(
            l, ctx, mod_ctx, cos_ctx, sin_ctx, w_in, tabs, kfin_ctx, zero_state,
            tm=tm_ctx, k_scale=k_scale_ctx)

        (q, qf, qb, kk, kfs, vv, sg, u, sbc, _, _) = _inproj(
            l, x, mod_lat, cos_lat, sin_lat, w_in, tabs, None, sb_fin,
            tm=tm_lat, k_scale=HEAD_DIM ** -0.5)
        x, _ = _mixer(l, q, qf, qb, kk, kfs, vv, sg, u, sbc, sf_fin, x, mod_lat, tabs, gn, cw, cv, w_out, ln1,
                      tm=tm_lat)
        x = _mlp(l, x, mod_lat, w_ff1, w_ff2, ln2, tm=tm_mlp)

        if not last:
            ctx, _ = _mixer(l, qc, qfc, qbc, kkc, kfsc, vvc, sgc, uc, sbc_c, zero_state, ctx, mod_ctx, tabs, gn,
                            cw, cv, w_out, ln1, tm=tm_ctx)
            ctx = _mlp(l, ctx, mod_ctx, w_ff1, w_ff2, ln2, tm=tm_ctx)

    return x
```

```python
import functools

import jax
import jax.numpy as jnp
from jax import lax
from jax.experimental import pallas as pl
from jax.experimental.pallas import tpu as pltpu

D_MODEL = 1024
DEPTH = 4
GRID_W = 64
RET_HEADS = 8
HEAD_DIM = 64
RET_WIDTH = RET_HEADS * HEAD_DIM
CONV_CH = 512
CONV_WIDTH = 31
CONV_HALO = 16
IN_WIDTH = 3072
D_FF = 4 * D_MODEL
CHUNK = 128
ROPE_BASE = 10000.0
LN_EPS = 1e-5
ALPHA = (2 * DEPTH) ** 0.25
LANES = 128
N_PAIRS = RET_HEADS // 2
FF_BLOCK = 1024
ROW_GROUP = 512
MLP_ROWS = 256
N_SLABS = CONV_CH // LANES
HALF = CHUNK // 2
Q_PLAIN, Q_FWD, Q_BWD = 0, 1, 2
K_PLAIN, K_FWD, V_PLAIN = 0, 1, 2
VMEM_LIMIT = 56 * 1024 * 1024

F32 = jnp.float32
BF16 = jnp.bfloat16


def _resident(shape):
    zeros = (0,) * len(shape)
    return pl.BlockSpec(shape, lambda *_: zeros, pipeline_mode=pl.Buffered(1))


def _layer_resident(layer, shape):
    zeros = (0,) * len(shape)
    return pl.BlockSpec((None,) + tuple(shape), lambda *_: (layer,) + zeros, pipeline_mode=pl.Buffered(1))


def _params():
    return pltpu.CompilerParams(dimension_semantics=("arbitrary", "arbitrary"),
                                vmem_limit_bytes=VMEM_LIMIT)


def _layer_norm(x, w, b):
    mu = jnp.mean(x, axis=-1, keepdims=True)
    d = x - mu
    var = jnp.mean(d * d, axis=-1, keepdims=True)
    return d * lax.rsqrt(var + LN_EPS) * w + b


def _pair_masks(rows):
    lane = lax.broadcasted_iota(jnp.int32, (rows, LANES), 1)
    return lane < HEAD_DIM, lane >= HEAD_DIM


def _block_diag_mask():
    r = lax.broadcasted_iota(jnp.int32, (LANES, LANES), 0)
    c = lax.broadcasted_iota(jnp.int32, (LANES, LANES), 1)
    return ((r < HEAD_DIM) == (c < HEAD_DIM)).astype(F32)


def _split_bf16(t):
    hi = t.astype(BF16)
    return hi, (t - hi.astype(F32)).astype(BF16)


def _adaln_kernel(cond_ref, w_ref, b_ref, o_ref):
    cond = cond_ref[...]
    a_hi, a_lo = _split_bf16(cond * jax.nn.sigmoid(cond))
    w_hi, w_lo = _split_bf16(w_ref[0])
    acc = jnp.dot(a_hi, w_lo, preferred_element_type=F32) + jnp.dot(a_lo, w_hi, preferred_element_type=F32)
    o_ref[0] = acc + jnp.dot(a_hi, w_hi, preferred_element_type=F32) + b_ref[0]


def _adaln(cond, w_ada, b_ada):
    width = w_ada.shape[-1]
    block = 1536
    return pl.pallas_call(
        _adaln_kernel,
        grid=(DEPTH, width // block),
        in_specs=[pl.BlockSpec((8, D_MODEL), lambda l, j: (0, 0)),
                  pl.BlockSpec((1, D_MODEL, block), lambda l, j: (l, 0, j)),
                  pl.BlockSpec((1, 1, block), lambda l, j: (l, 0, j))],
        out_specs=pl.BlockSpec((1, 8, block), lambda l, j: (l, 0, j)),
        out_shape=jax.ShapeDtypeStruct((DEPTH, 8, width), F32),
        compiler_params=_params(),
        name="adaln",
    )(cond, w_ada, b_ada.reshape(DEPTH, 1, width))


def _inproj_kernel(x_ref, mod_ref, cos_ref, sin_ref, w_ref, rf_ref, rb_ref, kdf_ref, kdb_ref, cdb_ref,
                   kfin_ref, sb0_ref,
                   qs_ref, ks_ref, sg_ref, u_ref, sbc_ref, sbfin_ref, sffin_ref,
                   kb_scr, kfin_scr, v_scr, sb_scr, sf_scr, *, tm, k_scale, want_final_fwd):
    t = pl.program_id(1)
    nt = pl.num_programs(1)
    cpt = tm // CHUNK

    @pl.when(t == 0)
    def _():
        sb_scr[...] = sb0_ref[0]
        sf_scr[...] = jnp.zeros_like(sf_scr)

    shift = mod_ref[0, 0:1, :]
    scale = mod_ref[0, 1:2, :]
    h = (x_ref[0] * (1.0 + scale) + shift).astype(BF16)

    def proj(col):
        return jnp.dot(h, w_ref[:, col:col + RET_WIDTH], preferred_element_type=F32)

    cos = cos_ref[...]
    sin = sin_ref[...]
    even = (lax.broadcasted_iota(jnp.int32, (tm, LANES), 1) % 2) == 0

    def rope(p):
        swapped = jnp.where(even, pltpu.roll(p, LANES - 1, 1), pltpu.roll(p, 1, 1))
        return p * cos + swapped * sin

    def put_stacked(which, j, t):
        left, right = _pair_masks(CHUNK)
        for c in range(cpt):
            tc = t[c * CHUNK:(c + 1) * CHUNK]
            ks_ref[0, which, j, 2 * c * CHUNK:(2 * c + 1) * CHUNK, :] = jnp.where(left, tc, 0.0).astype(BF16)
            ks_ref[0, which, j, (2 * c + 1) * CHUNK:(2 * c + 2) * CHUNK, :] = jnp.where(right, tc, 0.0).astype(BF16)

    pq = proj(0)
    pk = proj(RET_WIDTH)
    for j in range(N_PAIRS):
        lanes = slice(j * LANES, (j + 1) * LANES)
        qq = rope(pq[:, lanes])
        qs_ref[0, Q_PLAIN, j] = qq.astype(BF16)
        qs_ref[0, Q_FWD, j] = (qq * rf_ref[:, lanes]).astype(BF16)
        qs_ref[0, Q_BWD, j] = (qq * rb_ref[:, lanes]).astype(BF16)
        kk = rope(pk[:, lanes]) * k_scale
        put_stacked(K_PLAIN, j, kk)
        put_stacked(K_FWD, j, kk * kdf_ref[:, lanes])
        kb_scr[:, lanes] = (kk * kdb_ref[:, lanes]).astype(BF16)
        if want_final_fwd:
            kfin_scr[:, lanes] = (kk * kfin_ref[:, lanes]).astype(BF16)

    pv = proj(2 * RET_WIDTH)
    v_scr[...] = pv.astype(BF16)
    g = proj(3 * RET_WIDTH)
    sg = (g * jax.nn.sigmoid(g)).astype(BF16)
    for j in range(N_PAIRS):
        lanes = slice(j * LANES, (j + 1) * LANES)
        put_stacked(V_PLAIN, j, pv[:, lanes])
        sg_ref[0, j] = sg[:, lanes]
    a = proj(4 * RET_WIDTH)
    gate = proj(5 * RET_WIDTH)
    u_ref[0] = (a * jax.nn.sigmoid(gate)).astype(BF16)

    bd = _block_diag_mask()
    contract_rows = (((0,), (0,)), ((), ()))
    for c in reversed(range(cpt)):
        rows = slice(c * CHUNK, (c + 1) * CHUNK)
        for j in range(N_PAIRS):
            lanes = slice(j * LANES, (j + 1) * LANES)
            vp = v_scr[rows, lanes]
            s = sb_scr[j]
            sbc_ref[0, c, j] = s.astype(BF16)
            kv = lax.dot_general(kb_scr[rows, lanes], vp, contract_rows, preferred_element_type=F32)
            sb_scr[j] = s * cdb_ref[:, lanes] + kv * bd
            if want_final_fwd:
                kvf = lax.dot_general(kfin_scr[rows, lanes], vp, contract_rows,
                                      preferred_element_type=F32)
                sf_scr[j] = sf_scr[j] + kvf * bd

    @pl.when(t == nt - 1)
    def _():
        sbfin_ref[0] = sb_scr[...]
        sffin_ref[0] = sf_scr[...]


def _inproj(layer, x, mod, cos_t, sin_t, w_in, tabs, kfin, sb0, *, tm, k_scale):
    b, n, _ = x.shape
    nt = n // tm
    cpt = tm // CHUNK
    rev = lambda bb, t: (bb, nt - 1 - t, 0)
    rev_tab = lambda bb, t: (nt - 1 - t, 0)
    seq = jax.ShapeDtypeStruct((b, N_PAIRS, n, LANES), BF16)
    state = jax.ShapeDtypeStruct((b, N_PAIRS, LANES, LANES), F32)
    seq_spec = pl.BlockSpec((1, N_PAIRS, tm, LANES), lambda bb, t: (bb, 0, nt - 1 - t, 0))
    seq3 = jax.ShapeDtypeStruct((b, 3, N_PAIRS, n, LANES), BF16)
    seq3_spec = pl.BlockSpec((1, 3, N_PAIRS, tm, LANES), lambda bb, t: (bb, 0, 0, nt - 1 - t, 0))
    stk3 = jax.ShapeDtypeStruct((b, 3, N_PAIRS, 2 * n, LANES), BF16)
    stk3_spec = pl.BlockSpec((1, 3, N_PAIRS, 2 * tm, LANES), lambda bb, t: (bb, 0, 0, nt - 1 - t, 0))
    state_spec = pl.BlockSpec((1, N_PAIRS, LANES, LANES), lambda bb, t: (bb, 0, 0, 0))
    mod_spec = pl.BlockSpec((None, 1, 8, D_MODEL), lambda bb, t: (layer, bb if mod.shape[1] > 1 else 0, 0, 0))
    want_final_fwd = kfin is not None
    if want_final_fwd:
        kfin_spec = pl.BlockSpec((None, tm, RET_WIDTH), lambda bb, t: (layer, nt - 1 - t, 0))
        kfin_rows = tm
    else:
        kfin = jnp.zeros((8, RET_WIDTH), F32)
        kfin_spec = _resident((8, RET_WIDTH))
        kfin_rows = 16
    return pl.pallas_call(
        functools.partial(_inproj_kernel, tm=tm, k_scale=k_scale, want_final_fwd=want_final_fwd),
        grid=(b, nt),
        in_specs=[pl.BlockSpec((1, tm, D_MODEL), rev),
                  mod_spec,
                  pl.BlockSpec((tm, LANES), rev_tab),
                  pl.BlockSpec((tm, LANES), rev_tab),
                  _layer_resident(layer, (D_MODEL, IN_WIDTH)),
                  _layer_resident(layer, (tm, RET_WIDTH)),
                  _layer_resident(layer, (tm, RET_WIDTH)),
                  _layer_resident(layer, (tm, RET_WIDTH)),
                  _layer_resident(layer, (tm, RET_WIDTH)),
                  _layer_resident(layer, (1, RET_WIDTH)),
                  kfin_spec,
                  state_spec],
        out_specs=[seq3_spec, stk3_spec, seq_spec, pl.BlockSpec((1, tm, CONV_CH), rev),
                   pl.BlockSpec((1, cpt, N_PAIRS, LANES, LANES),
                                lambda bb, t: (bb, nt - 1 - t, 0, 0, 0)),
                   state_spec, state_spec],
        out_shape=[seq3, stk3, seq, jax.ShapeDtypeStruct((b, n, CONV_CH), BF16),
                   jax.ShapeDtypeStruct((b, n // CHUNK, N_PAIRS, LANES, LANES), BF16),
                   state, state],
        scratch_shapes=[pltpu.VMEM((tm, RET_WIDTH), BF16),
                        pltpu.VMEM((kfin_rows, RET_WIDTH), BF16),
                        pltpu.VMEM((tm, RET_WIDTH), BF16),
                        pltpu.VMEM((N_PAIRS, LANES, LANES), F32),
                        pltpu.VMEM((N_PAIRS, LANES, LANES), F32)],
        compiler_params=_params(),
        name="inproj",
    )(x, mod, cos_t, sin_t, w_in, tabs["rf"], tabs["rb"], tabs["kdf"], tabs["kdb"], tabs["cdb"], kfin, sb0)


def _mixer_kernel(qs_ref, ks_ref, sg_ref, u_ref, up_ref, un_ref, sbc_ref,
                  sf0_ref, x_ref, mod_ref, mask_ref, cdf_ref, gn_ref, cw_ref, cv_ref, ones_ref, wout_ref, ln_ref,
                  o_ref, sffin_ref,
                  sf_scr, sfb_scr, pr_scr, o_scr, uf_scr, y_scr, mix_scr, *, tm):
    t = pl.program_id(1)
    nt = pl.num_programs(1)
    cpt = tm // CHUNK
    n_iter = cpt * N_PAIRS // 2

    @pl.when(t == 0)
    def _():
        sf_scr[...] = sf0_ref[0]
        sfb_scr[...] = sf0_ref[0].astype(BF16)

    contract_rows = (((0,), (0,)), ((), ()))
    contract_lanes = (((1,), (1,)), ((), ()))

    def unit_of(it, e):
        c = it % cpt
        return (2 * (it // cpt) + e, pl.ds(pl.multiple_of(c * CHUNK, CHUNK), CHUNK),
                pl.ds(pl.multiple_of(2 * c * CHUNK, 2 * CHUNK), 2 * CHUNK))

    def probs_into(slot, it):
        for e in range(2):
            j, rows, srows = unit_of(it, e)
            sc = lax.dot_general(qs_ref[0, Q_PLAIN, j, rows, :], ks_ref[0, K_PLAIN, j, srows, :], contract_lanes,
                                 preferred_element_type=F32)
            pr_scr[slot, e] = (sc * mask_ref[j]).astype(BF16)

    def conv_unit(idx):
        slab = idx % N_SLABS
        base = (idx // N_SLABS) * CHUNK
        first = CONV_HALO - CONV_WIDTH // 2
        bias = jnp.broadcast_to(cw_ref[slab, CONV_WIDTH:CONV_WIDTH + 1, :], (HALF, LANES))
        accs = [[bias, None] for _ in range(2)]
        for tap in range(CONV_WIDTH):
            w = cw_ref[slab, tap:tap + 1, :]
            for par in range(2):
                term = w * uf_scr[slab, pl.ds(base + par + first + tap, HALF, stride=2), :]
                prev = accs[par][tap % 2]
                accs[par][tap % 2] = term if prev is None else prev + term
        for par in range(2):
            y_scr[slab, pl.ds(base + par, HALF, stride=2), :] = accs[par][0] + accs[par][1]

    probs_into(0, 0)

    for j in range(N_SLABS):
        lanes = slice(j * LANES, (j + 1) * LANES)
        uf_scr[j, 0:CONV_HALO, :] = jnp.where(t > 0, up_ref[0, :, lanes].astype(F32), 0.0)
        uf_scr[j, CONV_HALO:CONV_HALO + tm, :] = u_ref[0, :, lanes].astype(F32)
        uf_scr[j, CONV_HALO + tm:, :] = jnp.where(t < nt - 1, un_ref[0, :, lanes].astype(F32), 0.0)

    def step(it, carry):
        c = it % cpt
        units = [unit_of(it, e) for e in range(2)]
        probs = [pr_scr[it % 2, e] for e in range(2)]
        states = [sf_scr[j] for j, _, _ in units]
        states_b = [sfb_scr[j] for j, _, _ in units]
        new_states = []
        for e, (j, rows, srows) in enumerate(units):
            vv = ks_ref[0, V_PLAIN, j, srows, :]
            lhs = jnp.concatenate([probs[e], qs_ref[0, Q_FWD, j, rows, :], qs_ref[0, Q_BWD, j, rows, :]], axis=1)
            rhs = jnp.concatenate([vv, states_b[e], sbc_ref[0, c, j]], axis=0)
            o_scr[j, rows, :] = jnp.dot(lhs, rhs, preferred_element_type=F32)
            kv = lax.dot_general(ks_ref[0, K_FWD, j, srows, :], vv, contract_rows, preferred_element_type=F32)
            new_states.append(states[e] * cdf_ref[j] + kv)
        probs_into((it + 1) % 2, jnp.minimum(it + 1, n_iter - 1))
        for e, (j, _, _) in enumerate(units):
            sf_scr[j] = new_states[e]
            sfb_scr[j] = new_states[e].astype(BF16)
        conv_unit(2 * it)
        conv_unit(2 * it + 1)
        return carry

    lax.fori_loop(0, n_iter, step, 0)

    gate = mod_ref[0, 2:3, :]
    group = min(tm, ROW_GROUP)
    for g in range(tm // group):
        rows = slice(g * group, (g + 1) * group)
        for i in range(2):
            lanes = slice(i * 2 * LANES, (i + 1) * 2 * LANES)
            o = jnp.concatenate([o_scr[2 * i, rows, :], o_scr[2 * i + 1, rows, :]], axis=1)
            d = o - jnp.dot(o.astype(BF16), ones_ref[...], preferred_element_type=F32)
            var = jnp.dot((d * d).astype(BF16), ones_ref[...], preferred_element_type=F32)
            on = d * lax.rsqrt(var + LN_EPS) * gn_ref[0:1, lanes] + gn_ref[1:2, lanes]
            sg = jnp.concatenate([sg_ref[0, 2 * i, rows, :], sg_ref[0, 2 * i + 1, rows, :]], axis=1).astype(F32)
            mix_scr[rows, lanes] = (sg * on).astype(BF16)
        y = _layer_norm(jnp.concatenate([y_scr[j, rows, :] for j in range(N_SLABS)], axis=1),
                        cv_ref[0:1, :], cv_ref[1:2, :])
        mix_scr[rows, RET_WIDTH:] = (y * jax.nn.sigmoid(y)).astype(BF16)
        mix = jnp.dot(mix_scr[rows, :], wout_ref[...], preferred_element_type=F32)
        o_ref[0, rows, :] = _layer_norm(ALPHA * x_ref[0, rows, :] + gate * mix, ln_ref[0:1, :], ln_ref[1:2, :])

    @pl.when(t == nt - 1)
    def _():
        sffin_ref[0] = sf_scr[...]


def _group_ones():
    g = jnp.arange(2 * LANES) // HEAD_DIM
    return jnp.where(g[:, None] == g[None, :], 1.0 / HEAD_DIM, 0.0).astype(BF16)


def _mixer(layer, qs, ks, sg, u, sbc, sf0, x, mod, tabs, gn, cw, cv, w_out, ln1, *, tm):
    b, n, _ = x.shape
    nt = n // tm
    cpt = tm // CHUNK
    hpt = tm // CONV_HALO
    last_halo = n // CONV_HALO - 1
    tile = lambda bb, t: (bb, t, 0)
    seq_spec = pl.BlockSpec((1, N_PAIRS, tm, LANES), lambda bb, t: (bb, 0, t, 0))
    seq3_spec = pl.BlockSpec((1, 3, N_PAIRS, tm, LANES), lambda bb, t: (bb, 0, 0, t, 0))
    stk3_spec = pl.BlockSpec((1, 3, N_PAIRS, 2 * tm, LANES), lambda bb, t: (bb, 0, 0, t, 0))
    state_spec = pl.BlockSpec((1, N_PAIRS, LANES, LANES), lambda bb, t: (bb, 0, 0, 0))
    ones = _group_ones()
    return pl.pallas_call(
        functools.partial(_mixer_kernel, tm=tm),
        grid=(b, nt),
        in_specs=[seq3_spec, stk3_spec, seq_spec,
                  pl.BlockSpec((1, tm, CONV_CH), tile),
                  pl.BlockSpec((1, CONV_HALO, CONV_CH),
                               lambda bb, t: (bb, jnp.maximum(t * hpt - 1, 0), 0)),
                  pl.BlockSpec((1, CONV_HALO, CONV_CH),
                               lambda bb, t: (bb, jnp.minimum((t + 1) * hpt, last_halo), 0)),
                  pl.BlockSpec((1, cpt, N_PAIRS, LANES, LANES), lambda bb, t: (bb, t, 0, 0, 0)),
                  state_spec,
                  pl.BlockSpec((1, tm, D_MODEL), tile),
                  pl.BlockSpec((None, 1, 8, D_MODEL), lambda bb, t: (layer, bb if mod.shape[1] > 1 else 0, 0, 0)),
                  _layer_resident(layer, (N_PAIRS, CHUNK, 2 * CHUNK)),
                  _layer_resident(layer, (N_PAIRS, 1, LANES)),
                  _layer_resident(layer, (2, RET_WIDTH)),
                  _layer_resident(layer, (N_SLABS, 32, LANES)),
                  _layer_resident(layer, (2, CONV_CH)),
                  _resident(ones.shape),
                  _layer_resident(layer, (D_MODEL, D_MODEL)),
                  _layer_resident(layer, (2, D_MODEL))],
        out_specs=[pl.BlockSpec((1, tm, D_MODEL), tile), state_spec],
        out_shape=[jax.ShapeDtypeStruct((b, n, D_MODEL), F32),
                   jax.ShapeDtypeStruct((b, N_PAIRS, LANES, LANES), F32)],
        scratch_shapes=[pltpu.VMEM((N_PAIRS, LANES, LANES), F32),
                        pltpu.VMEM((N_PAIRS, LANES, LANES), BF16),
                        pltpu.VMEM((2, 2, CHUNK, 2 * CHUNK), BF16),
                        pltpu.VMEM((N_PAIRS, tm, LANES), F32),
                        pltpu.VMEM((N_SLABS, tm + 2 * CONV_HALO, LANES), F32),
                        pltpu.VMEM((N_SLABS, tm, LANES), F32),
                        pltpu.VMEM((tm, 2 * RET_WIDTH), BF16)],
        compiler_params=_params(),
        name="mixer",
    )(qs, ks, sg, u, u, u, sbc, sf0, x, mod, tabs["mask"], tabs["cdf"], gn, cw, cv, ones, w_out,
      ln1)


def _mlp_kernel(x_ref, mod_ref, w1_ref, w2_ref, ln_ref, o_ref, *, sub):
    for i in range(x_ref.shape[1] // sub):
        rows = slice(i * sub, (i + 1) * sub)
        x = x_ref[0, rows, :]
        h = (x * (1.0 + mod_ref[0, 4:5, :]) + mod_ref[0, 3:4, :]).astype(BF16)
        y = jnp.zeros(x.shape, F32)
        for j in range(D_FF // FF_BLOCK):
            cols = slice(j * FF_BLOCK, (j + 1) * FF_BLOCK)
            a = jnp.maximum(jnp.dot(h, w1_ref[:, cols], preferred_element_type=F32), 0.0)
            y = y + jnp.dot((a * a).astype(BF16), w2_ref[cols, :], preferred_element_type=F32)
        o_ref[0, rows, :] = _layer_norm(ALPHA * x + mod_ref[0, 5:6, :] * y, ln_ref[0:1, :], ln_ref[1:2, :])


def _mlp(layer, x, mod, w1, w2, ln2, *, tm):
    b, n, _ = x.shape
    tile = lambda bb, t: (bb, t, 0)
    return pl.pallas_call(
        functools.partial(_mlp_kernel, sub=min(tm, MLP_ROWS)),
        grid=(b, n // tm),
        in_specs=[pl.BlockSpec((1, tm, D_MODEL), tile),
                  pl.BlockSpec((None, 1, 8, D_MODEL), lambda bb, t: (layer, bb if mod.shape[1] > 1 else 0, 0, 0)),
                  _layer_resident(layer, (D_MODEL, D_FF)),
                  _layer_resident(layer, (D_FF, D_MODEL)),
                  _layer_resident(layer, (2, D_MODEL))],
        out_specs=pl.BlockSpec((1, tm, D_MODEL), tile),
        out_shape=jax.ShapeDtypeStruct((b, n, D_MODEL), F32),
        compiler_params=_params(),
        name="mlp",
    )(x, mod, w1, w2, ln2)


def _rope_tables(n):
    pos = jnp.arange(n, dtype=jnp.int32)
    row = (pos // GRID_W).astype(F32)[:, None]
    col = (pos % GRID_W).astype(F32)[:, None]
    n_freq = HEAD_DIM // 4
    lane = jnp.arange(LANES, dtype=jnp.int32)
    pair = (lane % HEAD_DIM) // 2
    inv = ROPE_BASE ** (-(pair % n_freq).astype(F32) / n_freq)
    ang = jnp.where(pair < n_freq, row, col) * inv
    sign = jnp.where(lane % 2 == 0, -1.0, 1.0)
    return jnp.cos(ang), jnp.sin(ang) * sign


def _per_lane(t):
    return jnp.repeat(jnp.swapaxes(t, 1, 2), HEAD_DIM, axis=2)


def _decay_tables(lg_f, lg_b, tm):
    idx = jnp.arange(CHUNK, dtype=F32)
    lf = lg_f.astype(F32)[:, :, None]
    lb = lg_b.astype(F32)[:, :, None]
    diff = idx[:, None] - idx[None, :]
    fwd = jnp.where(diff >= 0, jnp.exp(lf[..., None] * jnp.maximum(diff, 0.0)), 0.0)
    bwd = jnp.where(diff <= 0, jnp.exp(lb[..., None] * jnp.maximum(-diff, 0.0)), 0.0)
    mask = (fwd + bwd).reshape(DEPTH, N_PAIRS, 2, CHUNK, CHUNK).transpose(0, 1, 3, 2, 4)
    reps = tm // CHUNK
    tiled = lambda t: jnp.tile(_per_lane(t), (1, reps, 1))
    return {
        "mask": mask.reshape(DEPTH, N_PAIRS, CHUNK, 2 * CHUNK),
        "rf": tiled(jnp.exp(lf * (idx + 1.0))),
        "rb": tiled(jnp.exp(lb * (CHUNK - idx))),
        "kdf": tiled(jnp.exp(lf * (CHUNK - 1.0 - idx))),
        "kdb": tiled(jnp.exp(lb * idx)),
        "cdf": _per_lane(jnp.exp(lf * CHUNK)).reshape(DEPTH, N_PAIRS, 1, LANES),
        "cdb": _per_lane(jnp.exp(lb * CHUNK)),
    }


def _final_weights(lg_f, n):
    pos = jnp.arange(n, dtype=F32)
    return _per_lane(jnp.exp(lg_f.astype(F32)[:, :, None] * (n - 1.0 - pos)))


def kernel(x, c, ctx, c_ctx, w_ada, b_ada, w_in, ret_log_rate_fwd, ret_log_rate_bwd, ret_gn_w, ret_gn_b,
           conv_w, conv_b, conv_ln_w, conv_ln_b, w_out, ln1_w, ln1_b, w_ff1, w_ff2, ln2_w, ln2_b):
    b, n, _ = x.shape
    n_ctx = ctx.shape[1]
    tm_lat, tm_ctx, tm_mlp = 512, n_ctx, 1024
    w_in, w_out, w_ff1, w_ff2 = (w.astype(BF16) for w in (w_in, w_out, w_ff1, w_ff2))

    cond = jnp.zeros((8, D_MODEL), F32).at[:b].set(c).at[b].set(c_ctx)
    ada = _adaln(cond, w_ada, b_ada).reshape(DEPTH, 8, 6, D_MODEL)
    mods = jnp.pad(ada, ((0, 0), (0, 0), (0, 2), (0, 0)))
    mod_lat, mod_ctx = mods[:, :b], mods[:, b:b + 1]

    cos_lat, sin_lat = _rope_tables(n)
    cos_ctx = jnp.ones((n_ctx, LANES), F32)
    sin_ctx = jnp.zeros((n_ctx, LANES), F32)
    zero_state = jnp.zeros((b, N_PAIRS, LANES, LANES), F32)

    lg_f = -jnp.exp(ret_log_rate_fwd)
    lg_b = -jnp.exp(ret_log_rate_bwd)
    tabs = _decay_tables(lg_f, lg_b, tm_lat)
    kfin_ctx = _final_weights(lg_f, n_ctx)
    gn = jnp.stack([ret_gn_w, ret_gn_b], axis=1)
    cw = jnp.concatenate([conv_w, conv_b[:, None]], axis=1).reshape(DEPTH, 32, N_SLABS, LANES).transpose(0, 2, 1, 3)
    cv = jnp.stack([conv_ln_w, conv_ln_b], axis=1)
    ln1 = jnp.stack([ln1_w, ln1_b], axis=1)
    ln2 = jnp.stack([ln2_w, ln2_b], axis=1)

    for l in range(DEPTH):
        last = l == DEPTH - 1
        k_scale_ctx = 1.0 if last else HEAD_DIM ** -0.5
        (qsc, ksc, sgc, uc, sbc_c, sb_fin, sf_fin) = _inproj(
            l, ctx, mod_ctx, cos_ctx, sin_ctx, w_in, tabs, kfin_ctx, zero_state,
            tm=tm_ctx, k_scale=k_scale_ctx)

        (qs, ks, sg, u, sbc, _, _) = _inproj(
            l, x, mod_lat, cos_lat, sin_lat, w_in, tabs, None, sb_fin,
            tm=tm_lat, k_scale=HEAD_DIM ** -0.5)
        x, _ = _mixer(l, qs, ks, sg, u, sbc, sf_fin, x, mod_lat, tabs, gn, cw, cv, w_out, ln1, tm=tm_lat)
        x = _mlp(l, x, mod_lat, w_ff1, w_ff2, ln2, tm=tm_mlp)

        if not last:
            ctx, _ = _mixer(l, qsc, ksc, sgc, uc, sbc_c, zero_state, ctx, mod_ctx, tabs, gn, cw, cv, w_out, ln1,
                            tm=tm_ctx)
            ctx = _mlp(l, ctx, mod_ctx, w_ff1, w_ff2, ln2, tm=tm_ctx)

    return x
```

```python
import functools

import jax
import jax.numpy as jnp
from jax import lax
from jax.experimental import pallas as pl
from jax.experimental.pallas import tpu as pltpu

D_MODEL = 1024
DEPTH = 4
GRID_W = 64
RET_HEADS = 8
HEAD_DIM = 64
RET_WIDTH = RET_HEADS * HEAD_DIM
CONV_CH = 512
CONV_WIDTH = 31
CONV_HALO = 16
IN_WIDTH = 3072
D_FF = 4 * D_MODEL
CHUNK = 128
ROPE_BASE = 10000.0
LN_EPS = 1e-5
ALPHA = (2 * DEPTH) ** 0.25
LANES = 128
N_PAIRS = RET_HEADS // 2
FF_BLOCK = 1024
ROW_GROUP = 512
MLP_ROWS = 256
N_SLABS = CONV_CH // LANES
HALF = CHUNK // 2
Q_PLAIN, Q_FWD, Q_BWD = 0, 1, 2
K_PLAIN, K_FWD, V_PLAIN = 0, 1, 2
VMEM_LIMIT = 56 * 1024 * 1024

F32 = jnp.float32
BF16 = jnp.bfloat16


def _resident(shape):
    zeros = (0,) * len(shape)
    return pl.BlockSpec(shape, lambda *_: zeros, pipeline_mode=pl.Buffered(1))


def _layer_resident(layer, shape):
    zeros = (0,) * len(shape)
    return pl.BlockSpec((None,) + tuple(shape), lambda *_: (layer,) + zeros, pipeline_mode=pl.Buffered(1))


def _params():
    return pltpu.CompilerParams(dimension_semantics=("arbitrary", "arbitrary"),
                                vmem_limit_bytes=VMEM_LIMIT)


def _layer_norm(x, w, b):
    mu = jnp.mean(x, axis=-1, keepdims=True)
    d = x - mu
    var = jnp.mean(d * d, axis=-1, keepdims=True)
    return d * lax.rsqrt(var + LN_EPS) * w + b


def _pair_masks(rows):
    lane = lax.broadcasted_iota(jnp.int32, (rows, LANES), 1)
    return lane < HEAD_DIM, lane >= HEAD_DIM


def _block_diag_mask():
    r = lax.broadcasted_iota(jnp.int32, (LANES, LANES), 0)
    c = lax.broadcasted_iota(jnp.int32, (LANES, LANES), 1)
    return ((r < HEAD_DIM) == (c < HEAD_DIM)).astype(F32)


def _split_bf16(t):
    hi = t.astype(BF16)
    return hi, (t - hi.astype(F32)).astype(BF16)


def _adaln_kernel(cond_ref, w_ref, b_ref, o_ref):
    cond = cond_ref[...]
    a_hi, a_lo = _split_bf16(cond * jax.nn.sigmoid(cond))
    w_hi, w_lo = _split_bf16(w_ref[0])
    acc = jnp.dot(a_hi, w_lo, preferred_element_type=F32) + jnp.dot(a_lo, w_hi, preferred_element_type=F32)
    o_ref[0] = acc + jnp.dot(a_hi, w_hi, preferred_element_type=F32) + b_ref[0]


def _adaln(cond, w_ada, b_ada):
    width = w_ada.shape[-1]
    block = 1536
    return pl.pallas_call(
        _adaln_kernel,
        grid=(DEPTH, width // block),
        in_specs=[pl.BlockSpec((8, D_MODEL), lambda l, j: (0, 0)),
                  pl.BlockSpec((1, D_MODEL, block), lambda l, j: (l, 0, j)),
                  pl.BlockSpec((1, 1, block), lambda l, j: (l, 0, j))],
        out_specs=pl.BlockSpec((1, 8, block), lambda l, j: (l, 0, j)),
        out_shape=jax.ShapeDtypeStruct((DEPTH, 8, width), F32),
        compiler_params=_params(),
        name="adaln",
    )(cond, w_ada, b_ada.reshape(DEPTH, 1, width))


def _inproj_kernel(x_ref, mod_ref, cos_ref, sin_ref, w_ref, rf_ref, rb_ref, kdf_ref, kdb_ref, cdb_ref,
                   kfin_ref, sb0_ref,
                   qs_ref, ks_ref, sg_ref, u_ref, sbc_ref, sbfin_ref, sffin_ref,
                   kb_scr, kfin_scr, v_scr, sb_scr, sf_scr, *, tm, k_scale, want_final_fwd):
    t = pl.program_id(1)
    nt = pl.num_programs(1)
    cpt = tm // CHUNK

    @pl.when(t == 0)
    def _():
        sb_scr[...] = sb0_ref[0]
        sf_scr[...] = jnp.zeros_like(sf_scr)

    shift = mod_ref[0, 0:1, :]
    scale = mod_ref[0, 1:2, :]
    h = (x_ref[0] * (1.0 + scale) + shift).astype(BF16)

    def proj(col):
        return jnp.dot(h, w_ref[:, col:col + RET_WIDTH], preferred_element_type=F32)

    cos = cos_ref[...]
    sin = sin_ref[...]
    even = (lax.broadcasted_iota(jnp.int32, (tm, LANES), 1) % 2) == 0

    def rope(p):
        swapped = jnp.where(even, pltpu.roll(p, LANES - 1, 1), pltpu.roll(p, 1, 1))
        return p * cos + swapped * sin

    def put_stacked(which, j, t):
        left, right = _pair_masks(CHUNK)
        for c in range(cpt):
            tc = t[c * CHUNK:(c + 1) * CHUNK]
            ks_ref[0, which, j, 2 * c * CHUNK:(2 * c + 1) * CHUNK, :] = jnp.where(left, tc, 0.0).astype(BF16)
            ks_ref[0, which, j, (2 * c + 1) * CHUNK:(2 * c + 2) * CHUNK, :] = jnp.where(right, tc, 0.0).astype(BF16)

    pq = proj(0)
    pk = proj(RET_WIDTH)
    for j in range(N_PAIRS):
        lanes = slice(j * LANES, (j + 1) * LANES)
        qq = rope(pq[:, lanes])
        qs_ref[0, Q_PLAIN, j] = qq.astype(BF16)
        qs_ref[0, Q_FWD, j] = (qq * rf_ref[:, lanes]).astype(BF16)
        qs_ref[0, Q_BWD, j] = (qq * rb_ref[:, lanes]).astype(BF16)
        kk = rope(pk[:, lanes]) * k_scale
        put_stacked(K_PLAIN, j, kk)
        put_stacked(K_FWD, j, kk * kdf_ref[:, lanes])
        kb_scr[:, lanes] = (kk * kdb_ref[:, lanes]).astype(BF16)
        if want_final_fwd:
            kfin_scr[:, lanes] = (kk * kfin_ref[:, lanes]).astype(BF16)

    pv = proj(2 * RET_WIDTH)
    v_scr[...] = pv.astype(BF16)
    g = proj(3 * RET_WIDTH)
    sg = (g * jax.nn.sigmoid(g)).astype(BF16)
    for j in range(N_PAIRS):
        lanes = slice(j * LANES, (j + 1) * LANES)
        put_stacked(V_PLAIN, j, pv[:, lanes])
        sg_ref[0, j] = sg[:, lanes]
    a = proj(4 * RET_WIDTH)
    gate = proj(5 * RET_WIDTH)
    u_ref[0] = (a * jax.nn.sigmoid(gate)).astype(BF16)

    bd = _block_diag_mask()
    contract_rows = (((0,), (0,)), ((), ()))
    for c in reversed(range(cpt)):
        rows = slice(c * CHUNK, (c + 1) * CHUNK)
        for j in range(N_PAIRS):
            lanes = slice(j * LANES, (j + 1) * LANES)
            vp = v_scr[rows, lanes]
            s = sb_scr[j]
            sbc_ref[0, c, j] = s.astype(BF16)
            kv = lax.dot_general(kb_scr[rows, lanes], vp, contract_rows, preferred_element_type=F32)
            sb_scr[j] = s * cdb_ref[:, lanes] + kv * bd
            if want_final_fwd:
                kvf = lax.dot_general(kfin_scr[rows, lanes], vp, contract_rows,
                                      preferred_element_type=F32)
                sf_scr[j] = sf_scr[j] + kvf * bd

    @pl.when(t == nt - 1)
    def _():
        sbfin_ref[0] = sb_scr[...]
        sffin_ref[0] = sf_scr[...]


def _inproj(layer, x, mod, cos_t, sin_t, w_in, tabs, kfin, sb0, *, tm, k_scale):
    b, n, _ = x.shape
    nt = n // tm
    cpt = tm // CHUNK
    rev = lambda bb, t: (bb, nt - 1 - t, 0)
    rev_tab = lambda bb, t: (nt - 1 - t, 0)
    seq = jax.ShapeDtypeStruct((b, N_PAIRS, n, LANES), BF16)
    state = jax.ShapeDtypeStruct((b, N_PAIRS, LANES, LANES), F32)
    seq_spec = pl.BlockSpec((1, N_PAIRS, tm, LANES), lambda bb, t: (bb, 0, nt - 1 - t, 0))
    seq3 = jax.ShapeDtypeStruct((b, 3, N_PAIRS, n, LANES), BF16)
    seq3_spec = pl.BlockSpec((1, 3, N_PAIRS, tm, LANES), lambda bb, t: (bb, 0, 0, nt - 1 - t, 0))
    stk3 = jax.ShapeDtypeStruct((b, 3, N_PAIRS, 2 * n, LANES), BF16)
    stk3_spec = pl.BlockSpec((1, 3, N_PAIRS, 2 * tm, LANES), lambda bb, t: (bb, 0, 0, nt - 1 - t, 0))
    state_spec = pl.BlockSpec((1, N_PAIRS, LANES, LANES), lambda bb, t: (bb, 0, 0, 0))
    mod_spec = pl.BlockSpec((None, 1, 8, D_MODEL), lambda bb, t: (layer, bb if mod.shape[1] > 1 else 0, 0, 0))
    want_final_fwd = kfin is not None
    if want_final_fwd:
        kfin_spec = pl.BlockSpec((None, tm, RET_WIDTH), lambda bb, t: (layer, nt - 1 - t, 0))
        kfin_rows = tm
    else:
        kfin = jnp.zeros((8, RET_WIDTH), F32)
        kfin_spec = _resident((8, RET_WIDTH))
        kfin_rows = 16
    return pl.pallas_call(
        functools.partial(_inproj_kernel, tm=tm, k_scale=k_scale, want_final_fwd=want_final_fwd),
        grid=(b, nt),
        in_specs=[pl.BlockSpec((1, tm, D_MODEL), rev),
                  mod_spec,
                  pl.BlockSpec((tm, LANES), rev_tab),
                  pl.BlockSpec((tm, LANES), rev_tab),
                  _layer_resident(layer, (D_MODEL, IN_WIDTH)),
                  _layer_resident(layer, (tm, RET_WIDTH)),
                  _layer_resident(layer, (tm, RET_WIDTH)),
                  _layer_resident(layer, (tm, RET_WIDTH)),
                  _layer_resident(layer, (tm, RET_WIDTH)),
                  _layer_resident(layer, (1, RET_WIDTH)),
                  kfin_spec,
                  state_spec],
        out_specs=[seq3_spec, stk3_spec, seq_spec, pl.BlockSpec((1, tm, CONV_CH), rev),
                   pl.BlockSpec((1, cpt, N_PAIRS, LANES, LANES),
                                lambda bb, t: (bb, nt - 1 - t, 0, 0, 0)),
                   state_spec, state_spec],
        out_shape=[seq3, stk3, seq, jax.ShapeDtypeStruct((b, n, CONV_CH), BF16),
                   jax.ShapeDtypeStruct((b, n // CHUNK, N_PAIRS, LANES, LANES), BF16),
                   state, state],
        scratch_shapes=[pltpu.VMEM((tm, RET_WIDTH), BF16),
                        pltpu.VMEM((kfin_rows, RET_WIDTH), BF16),
                        pltpu.VMEM((tm, RET_WIDTH), BF16),
                        pltpu.VMEM((N_PAIRS, LANES, LANES), F32),
                        pltpu.VMEM((N_PAIRS, LANES, LANES), F32)],
        compiler_params=_params(),
        name="inproj",
    )(x, mod, cos_t, sin_t, w_in, tabs["rf"], tabs["rb"], tabs["kdf"], tabs["kdb"], tabs["cdb"], kfin, sb0)


def _mixer_kernel(qs_ref, ks_ref, sg_ref, u_ref, up_ref, un_ref, sbc_ref,
                  sf0_ref, x_ref, mod_ref, mask_ref, cdf_ref, gn_ref, cw_ref, cv_ref, ones_ref, wout_ref, ln_ref,
                  o_ref, sffin_ref,
                  sf_scr, sfb_scr, pr_scr, o_scr, uf_scr, y_scr, mix_scr, *, tm):
    t = pl.program_id(1)
    nt = pl.num_programs(1)
    cpt = tm // CHUNK
    n_iter = cpt * N_PAIRS // 2

    @pl.when(t == 0)
    def _():
        sf_scr[...] = sf0_ref[0]
        sfb_scr[...] = sf0_ref[0].astype(BF16)

    contract_rows = (((0,), (0,)), ((), ()))
    contract_lanes = (((1,), (1,)), ((), ()))

    def unit_of(it, e):
        c = it % cpt
        return (2 * (it // cpt) + e, pl.ds(pl.multiple_of(c * CHUNK, CHUNK), CHUNK),
                pl.ds(pl.multiple_of(2 * c * CHUNK, 2 * CHUNK), 2 * CHUNK))

    def probs_into(slot, it):
        for e in range(2):
            j, rows, srows = unit_of(it, e)
            sc = lax.dot_general(qs_ref[0, Q_PLAIN, j, rows, :], ks_ref[0, K_PLAIN, j, srows, :], contract_lanes,
                                 preferred_element_type=F32)
            pr_scr[slot, e] = (sc * mask_ref[j]).astype(BF16)

    def conv_unit(idx):
        slab = idx % N_SLABS
        base = (idx // N_SLABS) * CHUNK
        first = CONV_HALO - CONV_WIDTH // 2
        bias = jnp.broadcast_to(cw_ref[slab, CONV_WIDTH:CONV_WIDTH + 1, :], (HALF, LANES))
        accs = [[bias, None] for _ in range(2)]
        for tap in range(CONV_WIDTH):
            w = cw_ref[slab, tap:tap + 1, :]
            for par in range(2):
                term = w * uf_scr[slab, pl.ds(base + par + first + tap, HALF, stride=2), :]
                prev = accs[par][tap % 2]
                accs[par][tap % 2] = term if prev is None else prev + term
        for par in range(2):
            y_scr[slab, pl.ds(base + par, HALF, stride=2), :] = accs[par][0] + accs[par][1]

    probs_into(0, 0)

    for j in range(N_SLABS):
        lanes = slice(j * LANES, (j + 1) * LANES)
        uf_scr[j, 0:CONV_HALO, :] = jnp.where(t > 0, up_ref[0, :, lanes].astype(F32), 0.0)
        uf_scr[j, CONV_HALO:CONV_HALO + tm, :] = u_ref[0, :, lanes].astype(F32)
        uf_scr[j, CONV_HALO + tm:, :] = jnp.where(t < nt - 1, un_ref[0, :, lanes].astype(F32), 0.0)

    def step(it, carry):
        c = it % cpt
        units = [unit_of(it, e) for e in range(2)]
        probs = [pr_scr[it % 2, e] for e in range(2)]
        states = [sf_scr[j] for j, _, _ in units]
        states_b = [sfb_scr[j] for j, _, _ in units]
        new_states = []
        for e, (j, rows, srows) in enumerate(units):
            vv = ks_ref[0, V_PLAIN, j, srows, :]
            lhs = jnp.concatenate([probs[e], qs_ref[0, Q_FWD, j, rows, :], qs_ref[0, Q_BWD, j, rows, :]], axis=1)
            rhs = jnp.concatenate([vv, states_b[e], sbc_ref[0, c, j]], axis=0)
            o_scr[j, rows, :] = jnp.dot(lhs, rhs, preferred_element_type=F32)
            kv = lax.dot_general(ks_ref[0, K_FWD, j, srows, :], vv, contract_rows, preferred_element_type=F32)
            new_states.append(states[e] * cdf_ref[j] + kv)
        probs_into((it + 1) % 2, jnp.minimum(it + 1, n_iter - 1))
        for e, (j, _, _) in enumerate(units):
            sf_scr[j] = new_states[e]
            sfb_scr[j] = new_states[e].astype(BF16)
        conv_unit(2 * it)
        conv_unit(2 * it + 1)
        return carry

    lax.fori_loop(0, n_iter, step, 0)

    gate = mod_ref[0, 2:3, :]
    group = min(tm, ROW_GROUP)
    for g in range(tm // group):
        rows = slice(g * group, (g + 1) * group)
        for i in range(2):
            lanes = slice(i * 2 * LANES, (i + 1) * 2 * LANES)
            o = jnp.concatenate([o_scr[2 * i, rows, :], o_scr[2 * i + 1, rows, :]], axis=1)
            d = o - jnp.dot(o.astype(BF16), ones_ref[...], preferred_element_type=F32)
            var = jnp.dot((d * d).astype(BF16), ones_ref[...], preferred_element_type=F32)
            on = d * lax.rsqrt(var + LN_EPS) * gn_ref[0:1, lanes] + gn_ref[1:2, lanes]
            sg = jnp.concatenate([sg_ref[0, 2 * i, rows, :], sg_ref[0, 2 * i + 1, rows, :]], axis=1).astype(F32)
            mix_scr[rows, lanes] = (sg * on).astype(BF16)
        y = _layer_norm(jnp.concatenate([y_scr[j, rows, :] for j in range(N_SLABS)], axis=1),
                        cv_ref[0:1, :], cv_ref[1:2, :])
        mix_scr[rows, RET_WIDTH:] = (y * jax.nn.sigmoid(y)).astype(BF16)
        mix = jnp.dot(mix_scr[rows, :], wout_ref[...], preferred_element_type=F32)
        o_ref[0, rows, :] = _layer_norm(ALPHA * x_ref[0, rows, :] + gate * mix, ln_ref[0:1, :], ln_ref[1:2, :])

    @pl.when(t == nt - 1)
    def _():
        sffin_ref[0] = sf_scr[...]


def _group_ones():
    g = jnp.arange(2 * LANES) // HEAD_DIM
    return jnp.where(g[:, None] == g[None, :], 1.0 / HEAD_DIM, 0.0).astype(BF16)


def _mixer(layer, qs, ks, sg, u, sbc, sf0, x, mod, tabs, gn, cw, cv, w_out, ln1, *, tm):
    b, n, _ = x.shape
    nt = n // tm
    cpt = tm // CHUNK
    hpt = tm // CONV_HALO
    last_halo = n // CONV_HALO - 1
    tile = lambda bb, t: (bb, t, 0)
    seq_spec = pl.BlockSpec((1, N_PAIRS, tm, LANES), lambda bb, t: (bb, 0, t, 0))
    seq3_spec = pl.BlockSpec((1, 3, N_PAIRS, tm, LANES), lambda bb, t: (bb, 0, 0, t, 0))
    stk3_spec = pl.BlockSpec((1, 3, N_PAIRS, 2 * tm, LANES), lambda bb, t: (bb, 0, 0, t, 0))
    state_spec = pl.BlockSpec((1, N_PAIRS, LANES, LANES), lambda bb, t: (bb, 0, 0, 0))
    ones = _group_ones()
    return pl.pallas_call(
        functools.partial(_mixer_kernel, tm=tm),
        grid=(b, nt),
        in_specs=[seq3_spec, stk3_spec, seq_spec,
                  pl.BlockSpec((1, tm, CONV_CH), tile),
                  pl.BlockSpec((1, CONV_HALO, CONV_CH),
                               lambda bb, t: (bb, jnp.maximum(t * hpt - 1, 0), 0)),
                  pl.BlockSpec((1, CONV_HALO, CONV_CH),
                               lambda bb, t: (bb, jnp.minimum((t + 1) * hpt, last_halo), 0)),
                  pl.BlockSpec((1, cpt, N_PAIRS, LANES, LANES), lambda bb, t: (bb, t, 0, 0, 0)),
                  state_spec,
                  pl.BlockSpec((1, tm, D_MODEL), tile),
                  pl.BlockSpec((None, 1, 8, D_MODEL), lambda bb, t: (layer, bb if mod.shape[1] > 1 else 0, 0, 0)),
                  _layer_resident(layer, (N_PAIRS, CHUNK, 2 * CHUNK)),
                  _layer_resident(layer, (N_PAIRS, 1, LANES)),
                  _layer_resident(layer, (2, RET_WIDTH)),
                  _layer_resident(layer, (N_SLABS, 32, LANES)),
                  _layer_resident(layer, (2, CONV_CH)),
                  _resident(ones.shape),
                  _layer_resident(layer, (D_MODEL, D_MODEL)),
                  _layer_resident(layer, (2, D_MODEL))],
        out_specs=[pl.BlockSpec((1, tm, D_MODEL), tile), state_spec],
        out_shape=[jax.ShapeDtypeStruct((b, n, D_MODEL), F32),
                   jax.ShapeDtypeStruct((b, N_PAIRS, LANES, LANES), F32)],
        scratch_shapes=[pltpu.VMEM((N_PAIRS, LANES, LANES), F32),
                        pltpu.VMEM((N_PAIRS, LANES, LANES), BF16),
                        pltpu.VMEM((2, 2, CHUNK, 2 * CHUNK), BF16),
                        pltpu.VMEM((N_PAIRS, tm, LANES), F32),
                        pltpu.VMEM((N_SLABS, tm + 2 * CONV_HALO, LANES), F32),
                        pltpu.VMEM((N_SLABS, tm, LANES), F32),
                        pltpu.VMEM((tm, 2 * RET_WIDTH), BF16)],
        compiler_params=_params(),
        name="mixer",
    )(qs, ks, sg, u, u, u, sbc, sf0, x, mod, tabs["mask"], tabs["cdf"], gn, cw, cv, ones, w_out,
      ln1)


def _mlp_kernel(x_ref, mod_ref, w1_ref, w2_ref, ln_ref, o_ref, *, sub):
    for i in range(x_ref.shape[1] // sub):
        rows = slice(i * sub, (i + 1) * sub)
        x = x_ref[0, rows, :]
        h = (x * (1.0 + mod_ref[0, 4:5, :]) + mod_ref[0, 3:4, :]).astype(BF16)
        y = jnp.zeros(x.shape, F32)
        for j in range(D_FF // FF_BLOCK):
            cols = slice(j * FF_BLOCK, (j + 1) * FF_BLOCK)
            a = jnp.maximum(jnp.dot(h, w1_ref[:, cols], preferred_element_type=F32), 0.0)
            y = y + jnp.dot((a * a).astype(BF16), w2_ref[cols, :], preferred_element_type=F32)
        o_ref[0, rows, :] = _layer_norm(ALPHA * x + mod_ref[0, 5:6, :] * y, ln_ref[0:1, :], ln_ref[1:2, :])


def _mlp(layer, x, mod, w1, w2, ln2, *, tm):
    b, n, _ = x.shape
    tile = lambda bb, t: (bb, t, 0)
    return pl.pallas_call(
        functools.partial(_mlp_kernel, sub=min(tm, MLP_ROWS)),
        grid=(b, n // tm),
        in_specs=[pl.BlockSpec((1, tm, D_MODEL), tile),
                  pl.BlockSpec((None, 1, 8, D_MODEL), lambda bb, t: (layer, bb if mod.shape[1] > 1 else 0, 0, 0)),
                  _layer_resident(layer, (D_MODEL, D_FF)),
                  _layer_resident(layer, (D_FF, D_MODEL)),
                  _layer_resident(layer, (2, D_MODEL))],
        out_specs=pl.BlockSpec((1, tm, D_MODEL), tile),
        out_shape=jax.ShapeDtypeStruct((b, n, D_MODEL), F32),
        compiler_params=_params(),
        name="mlp",
    )(x, mod, w1, w2, ln2)


def _rope_tables(n):
    pos = jnp.arange(n, dtype=jnp.int32)
    row = (pos // GRID_W).astype(F32)[:, None]
    col = (pos % GRID_W).astype(F32)[:, None]
    n_freq = HEAD_DIM // 4
    lane = jnp.arange(LANES, dtype=jnp.int32)
    pair = (lane % HEAD_DIM) // 2
    inv = ROPE_BASE ** (-(pair % n_freq).astype(F32) / n_freq)
    ang = jnp.where(pair < n_freq, row, col) * inv
    sign = jnp.where(lane % 2 == 0, -1.0, 1.0)
    return jnp.cos(ang), jnp.sin(ang) * sign


def _per_lane(t):
    return jnp.repeat(jnp.swapaxes(t, 1, 2), HEAD_DIM, axis=2)


def _decay_tables(lg_f, lg_b, tm):
    idx = jnp.arange(CHUNK, dtype=F32)
    lf = lg_f.astype(F32)[:, :, None]
    lb = lg_b.astype(F32)[:, :, None]
    diff = idx[:, None] - idx[None, :]
    fwd = jnp.where(diff >= 0, jnp.exp(lf[..., None] * jnp.maximum(diff, 0.0)), 0.0)
    bwd = jnp.where(diff <= 0, jnp.exp(lb[..., None] * jnp.maximum(-diff, 0.0)), 0.0)
    mask = (fwd + bwd).reshape(DEPTH, N_PAIRS, 2, CHUNK, CHUNK).transpose(0, 1, 3, 2, 4)
    reps = tm // CHUNK
    tiled = lambda t: jnp.tile(_per_lane(t), (1, reps, 1))
    return {
        "mask": mask.reshape(DEPTH, N_PAIRS, CHUNK, 2 * CHUNK),
        "rf": tiled(jnp.exp(lf * (idx + 1.0))),
        "rb": tiled(jnp.exp(lb * (CHUNK - idx))),
        "kdf": tiled(jnp.exp(lf * (CHUNK - 1.0 - idx))),
        "kdb": tiled(jnp.exp(lb * idx)),
        "cdf": _per_lane(jnp.exp(lf * CHUNK)).reshape(DEPTH, N_PAIRS, 1, LANES),
        "cdb": _per_lane(jnp.exp(lb * CHUNK)),
    }


def _final_weights(lg_f, n):
    pos = jnp.arange(n, dtype=F32)
    return _per_lane(jnp.exp(lg_f.astype(F32)[:, :, None] * (n - 1.0 - pos)))


def kernel(x, c, ctx, c_ctx, w_ada, b_ada, w_in, ret_log_rate_fwd, ret_log_rate_bwd, ret_gn_w, ret_gn_b,
           conv_w, conv_b, conv_ln_w, conv_ln_b, w_out, ln1_w, ln1_b, w_ff1, w_ff2, ln2_w, ln2_b):
    b, n, _ = x.shape
    n_ctx = ctx.shape[1]
    tm_lat, tm_ctx, tm_mlp, tm_mix = 512, n_ctx, 1024, 1024
    w_in, w_out, w_ff1, w_ff2 = (w.astype(BF16) for w in (w_in, w_out, w_ff1, w_ff2))

    cond = jnp.zeros((8, D_MODEL), F32).at[:b].set(c).at[b].set(c_ctx)
    ada = _adaln(cond, w_ada, b_ada).reshape(DEPTH, 8, 6, D_MODEL)
    mods = jnp.pad(ada, ((0, 0), (0, 0), (0, 2), (0, 0)))
    mod_lat, mod_ctx = mods[:, :b], mods[:, b:b + 1]

    cos_lat, sin_lat = _rope_tables(n)
    cos_ctx = jnp.ones((n_ctx, LANES), F32)
    sin_ctx = jnp.zeros((n_ctx, LANES), F32)
    zero_state = jnp.zeros((b, N_PAIRS, LANES, LANES), F32)

    lg_f = -jnp.exp(ret_log_rate_fwd)
    lg_b = -jnp.exp(ret_log_rate_bwd)
    tabs = _decay_tables(lg_f, lg_b, tm_lat)
    kfin_ctx = _final_weights(lg_f, n_ctx)
    gn = jnp.stack([ret_gn_w, ret_gn_b], axis=1)
    cw = jnp.concatenate([conv_w, conv_b[:, None]], axis=1).reshape(DEPTH, 32, N_SLABS, LANES).transpose(0, 2, 1, 3)
    cv = jnp.stack([conv_ln_w, conv_ln_b], axis=1)
    ln1 = jnp.stack([ln1_w, ln1_b], axis=1)
    ln2 = jnp.stack([ln2_w, ln2_b], axis=1)

    for l in range(DEPTH):
        last = l == DEPTH - 1
        k_scale_ctx = 1.0 if last else HEAD_DIM ** -0.5
        (qsc, ksc, sgc, uc, sbc_c, sb_fin, sf_fin) = _inproj(
            l, ctx, mod_ctx, cos_ctx, sin_ctx, w_in, tabs, kfin_ctx, zero_state,
            tm=tm_ctx, k_scale=k_scale_ctx)

        (qs, ks, sg, u, sbc, _, _) = _inproj(
            l, x, mod_lat, cos_lat, sin_lat, w_in, tabs, None, sb_fin,
            tm=tm_lat, k_scale=HEAD_DIM ** -0.5)
        x, _ = _mixer(l, qs, ks, sg, u, sbc, sf_fin, x, mod_lat, tabs, gn, cw, cv, w_out, ln1, tm=tm_mix)
        x = _mlp(l, x, mod_lat, w_ff1, w_ff2, ln2, tm=tm_mlp)

        if not last:
            ctx, _ = _mixer(l, qsc, ksc, sgc, uc, sbc_c, zero_state, ctx, mod_ctx, tabs, gn, cw, cv, w_out, ln1,
                            tm=tm_ctx)
            ctx = _mlp(l, ctx, mod_ctx, w_ff1, w_ff2, ln2, tm=tm_ctx)

    return x
```
